```python
import jax, jax.numpy as jnp
from jax import lax
import numpy as np

D_MODEL = 2048
BATCH = 8
SEQ = 2048
DEPTH = 1
DEC_BATCH = 2
DEC_SEQ = 4096
PAST_LEN = 128

HEAD_DIM = 128
A_HEADS = 8
A_WIDTH = A_HEADS * HEAD_DIM
B_HEADS = 8
B_KV_HEADS = 2
B_GROUP = B_HEADS // B_KV_HEADS
B_WIDTH = B_HEADS * HEAD_DIM
B_KV_WIDTH = B_KV_HEADS * HEAD_DIM
MIX_WIDTH = A_WIDTH + B_WIDTH
QKV_WIDTH = 3 * A_WIDTH + B_WIDTH + 2 * B_KV_WIDTH
DILATED_CONFIGS = ((128, 1), (512, 4), (2048, 16))
B_RADIUS = 128
ROPE_THETA = 500000.0
ROPE_DIM = HEAD_DIM // 4
N_GROUPS = 4
EXPERTS_PER_GROUP = 8
N_EXPERTS = N_GROUPS * EXPERTS_PER_GROUP
TOP_K_IN_GROUP = 2
D_EXPERT = 256
PLE_DIM = 256
RMS_EPS = 1e-6
NEG_INF = -1e30

kernel_name = "hymba_dilated_swa_hmoe_encoder"


def rmsnorm(x, g):
    xf = x.astype(jnp.float32)
    xf = xf * lax.rsqrt(jnp.mean(xf * xf, axis=-1, keepdims=True) + RMS_EPS)
    return (xf * g.astype(jnp.float32)).astype(x.dtype)


def partial_rope(x, pos):
    half = ROPE_DIM // 2
    inv = ROPE_THETA ** (-(jnp.arange(half, dtype=jnp.float32) * 2.0) / ROPE_DIM)
    ang = pos[:, None] * inv[None, :]
    cos = jnp.cos(ang)[None, :, None, :]
    sin = jnp.sin(ang)[None, :, None, :]
    xf = x.astype(jnp.float32)
    x1 = xf[..., :half]
    x2 = xf[..., half:ROPE_DIM]
    out = jnp.concatenate([x1 * cos - x2 * sin, x2 * cos + x1 * sin, xf[..., ROPE_DIM:]], axis=-1)
    return out.astype(x.dtype)


def band_attention(q, k, v, radius, block, sink=None):
    N, L, Hkv, G, dh = q.shape
    nb = -(-L // block)
    Lp = nb * block
    qp = jnp.pad(q, ((0, 0), (0, Lp - L), (0, 0), (0, 0), (0, 0))).reshape(N, nb, block, Hkv, G, dh)
    kpad = ((0, 0), (block, Lp - L + block), (0, 0), (0, 0))
    kp = jnp.pad(k, kpad).reshape(N, nb + 2, block, Hkv, dh)
    vp = jnp.pad(v, kpad).reshape(N, nb + 2, block, Hkv, dh)
    kb = jnp.concatenate([kp[:, :-2], kp[:, 1:-1], kp[:, 2:]], axis=2)
    vb = jnp.concatenate([vp[:, :-2], vp[:, 1:-1], vp[:, 2:]], axis=2)
    blk = jnp.arange(nb)[:, None]
    qpos = blk * block + jnp.arange(block)[None, :]
    kpos = (blk - 1) * block + jnp.arange(3 * block)[None, :]
    rel = kpos[:, None, :] - qpos[:, :, None]
    valid = (jnp.abs(rel) <= radius) & (kpos[:, None, :] >= 0) & (kpos[:, None, :] < L)
    s = jnp.einsum('nbqhgd,nbkhd->nbhgqk', qp.astype(jnp.float32), kb.astype(jnp.float32)) * (dh ** -0.5)
    s = jnp.where(valid[None, :, None, None], s, NEG_INF)
    m = jnp.max(s, axis=-1)
    if sink is not None:
        sk = sink.astype(jnp.float32).reshape(Hkv, G)[None, None, :, :, None]
        m = jnp.maximum(m, sk)
    pr = jnp.exp(s - m[..., None])
    denom = jnp.sum(pr, axis=-1)
    if sink is not None:
        denom = denom + jnp.exp(sk - m)
    o = jnp.einsum('nbhgqk,nbkhd->nbqhgd', pr, vb.astype(jnp.float32))
    denom_t = jnp.moveaxis(denom, -1, 2)
    m_t = jnp.moveaxis(m, -1, 2)
    o = (o / denom_t[..., None]).reshape(N, Lp, Hkv, G, dh)[:, :L]
    lse = (m_t + jnp.log(denom_t)).reshape(N, Lp, Hkv, G)[:, :L]
    return o, lse


def dilated_attention(q, k, v):
    B, S, H, dh = q.shape
    outs, lses = [], []
    for window, dil in DILATED_CONFIGS:
        radius = window // (2 * dil)
        Ls = S // dil

        def split(t):
            return t.reshape(B, Ls, dil, H, dh).transpose(0, 2, 1, 3, 4).reshape(B * dil, Ls, H, dh)

        o, lse = band_attention(split(q)[:, :, :, None], split(k), split(v), radius, radius)
        o = o[:, :, :, 0].reshape(B, dil, Ls, H, dh).transpose(0, 2, 1, 3, 4).reshape(B, S, H, dh)
        lse = lse[:, :, :, 0].reshape(B, dil, Ls, H).transpose(0, 2, 1, 3).reshape(B, S, H)
        outs.append(o)
        lses.append(lse)
    w = jax.nn.softmax(jnp.stack(lses, axis=0), axis=0)
    return jnp.sum(w[..., None] * jnp.stack(outs, axis=0), axis=0)


def token_mixer(h, g_attn, w_in, g_out_a, g_out_b, sink, w_o):
    B, S, _ = h.shape
    u = rmsnorm(h, g_attn)
    qkv = u @ w_in
    cuts = [int(c) for c in np.cumsum([A_WIDTH, A_WIDTH, A_WIDTH, B_WIDTH, B_KV_WIDTH])]
    qa, ka, va, qb, kb, vb = jnp.split(qkv, cuts, axis=-1)
    pos = jnp.arange(S, dtype=jnp.float32)
    qa = partial_rope(qa.reshape(B, S, A_HEADS, HEAD_DIM), pos)
    ka = partial_rope(ka.reshape(B, S, A_HEADS, HEAD_DIM), pos)
    va = va.reshape(B, S, A_HEADS, HEAD_DIM)
    oa = dilated_attention(qa, ka, va).reshape(B, S, A_WIDTH).astype(h.dtype)
    qb = partial_rope(qb.reshape(B, S, B_HEADS, HEAD_DIM), pos).reshape(B, S, B_KV_HEADS, B_GROUP, HEAD_DIM)
    kb = partial_rope(kb.reshape(B, S, B_KV_HEADS, HEAD_DIM), pos)
    vb = vb.reshape(B, S, B_KV_HEADS, HEAD_DIM)
    ob, _ = band_attention(qb, kb, vb, B_RADIUS, B_RADIUS, sink)
    ob = ob.reshape(B, S, B_WIDTH).astype(h.dtype)
    mixed = jnp.concatenate([rmsnorm(oa, g_out_a), rmsnorm(ob, g_out_b)], axis=-1)
    return h + mixed @ w_o


def hierarchical_moe(h, g_ffn, w_group, b_group, w_router, b_router, w1, w3, w2):
    B, S, D = h.shape
    T = B * S
    u = rmsnorm(h, g_ffn).reshape(T, D)
    gl = (u @ w_group).astype(jnp.float32) + b_group.astype(jnp.float32)
    gp = jax.nn.softmax(gl, axis=-1)
    gidx = jnp.argmax(gl, axis=-1)
    ggate = jnp.take_along_axis(gp, gidx[:, None], axis=-1)
    el = ((u @ w_router).astype(jnp.float32) + b_router.astype(jnp.float32)).reshape(T, N_GROUPS, EXPERTS_PER_GROUP)
    el = jnp.take_along_axis(el, gidx[:, None, None], axis=1)[:, 0]
    tv, ti = lax.top_k(el, TOP_K_IN_GROUP)
    tw = jax.nn.softmax(tv, axis=-1) * ggate
    eidx = gidx[:, None] * EXPERTS_PER_GROUP + ti
    comb = jnp.sum(jax.nn.one_hot(eidx, N_EXPERTS, dtype=jnp.float32) * tw[..., None], axis=1)
    a = jnp.einsum('td,edf->tef', u, w1)
    b = jnp.einsum('td,edf->tef', u, w3)
    hid = jax.nn.silu(a) * b * comb.astype(u.dtype)[..., None]
    y = jnp.einsum('tef,efd->td', hid, w2)
    return h + y.reshape(B, S, D)


def per_layer_embed(h, p, g_ple, w_ple_gate, b_ple_gate, w_ple_proj):
    gate = jax.nn.sigmoid((rmsnorm(h, g_ple) @ w_ple_gate).astype(jnp.float32) + b_ple_gate.astype(jnp.float32))
    return h + (gate * (p @ w_ple_proj).astype(jnp.float32)).astype(h.dtype)


def trunk(x, p, g_attn, w_in, g_out_a, g_out_b, sink, w_o, g_ffn, w_group, b_group, w_router,
          b_router, w1, w3, w2, g_ple, w_ple_gate, b_ple_gate, w_ple_proj, g_final):
    h = x
    for l in range(DEPTH):
        h = token_mixer(h, g_attn[l], w_in[l], g_out_a[l], g_out_b[l], sink[l], w_o[l])
        h = hierarchical_moe(h, g_ffn[l], w_group[l], b_group[l], w_router[l], b_router[l], w1[l], w3[l], w2[l])
        h = per_layer_embed(h, p[l], g_ple[l], w_ple_gate[l], b_ple_gate[l], w_ple_proj[l])
    return rmsnorm(h, g_final)


def setup_inputs(seed: int = 0) -> dict:
    key = jax.random.key(seed)
    ks = jax.random.split(key, 24)
    f32 = jnp.float32
    D = D_MODEL

    def nrm(k, shape, scale):
        return jax.random.normal(k, shape, f32) * scale

    def gain(k, shape):
        return 1.0 + 0.02 * jax.random.normal(k, shape, f32)

    return {
        "x_prompt": nrm(ks[0], (BATCH, SEQ, D), 1.0),
        "x_sample": nrm(ks[1], (DEC_BATCH, DEC_SEQ, D), 1.0),
        "p_prompt": nrm(ks[2], (DEPTH, BATCH, SEQ, PLE_DIM), 1.0),
        "p_sample": nrm(ks[3], (DEPTH, DEC_BATCH, DEC_SEQ, PLE_DIM), 1.0),
        "g_attn": gain(ks[4], (DEPTH, D)),
        "w_in": nrm(ks[5], (DEPTH, D, QKV_WIDTH), D ** -0.5),
        "g_out_a": gain(ks[6], (DEPTH, A_WIDTH)),
        "g_out_b": gain(ks[7], (DEPTH, B_WIDTH)),
        "sink": nrm(ks[8], (DEPTH, B_HEADS), 1.0),
        "w_o": nrm(ks[9], (DEPTH, MIX_WIDTH, D), MIX_WIDTH ** -0.5),
        "g_ffn": gain(ks[10], (DEPTH, D)),
        "w_group": nrm(ks[11], (DEPTH, D, N_GROUPS), D ** -0.5),
        "b_group": nrm(ks[12], (DEPTH, N_GROUPS), 0.01),
        "w_router": nrm(ks[13], (DEPTH, D, N_EXPERTS), D ** -0.5),
        "b_router": nrm(ks[14], (DEPTH, N_EXPERTS), 0.01),
        "w1": nrm(ks[15], (DEPTH, N_EXPERTS, D, D_EXPERT), D ** -0.5),
        "w3": nrm(ks[16], (DEPTH, N_EXPERTS, D, D_EXPERT), D ** -0.5),
        "w2": nrm(ks[17], (DEPTH, N_EXPERTS, D_EXPERT, D), D_EXPERT ** -0.5),
        "g_ple": gain(ks[18], (DEPTH, D)),
        "w_ple_gate": nrm(ks[19], (DEPTH, D, D), D ** -0.5),
        "b_ple_gate": nrm(ks[20], (DEPTH, D), 0.01),
        "w_ple_proj": nrm(ks[21], (DEPTH, PLE_DIM, D), PLE_DIM ** -0.5),
        "g_final": gain(ks[22], (D,)),
    }


def reference(x_prompt, x_sample, p_prompt, p_sample, g_attn, w_in, g_out_a, g_out_b, sink, w_o,
              g_ffn, w_group, b_group, w_router, b_router, w1, w3, w2, g_ple, w_ple_gate, b_ple_gate,
              w_ple_proj, g_final):
    y_prompt = trunk(x_prompt, p_prompt, g_attn, w_in, g_out_a, g_out_b, sink, w_o, g_ffn, w_group,
                     b_group, w_router, b_router, w1, w3, w2, g_ple, w_ple_gate, b_ple_gate, w_ple_proj, g_final)
    y_sample = trunk(x_sample, p_sample, g_attn, w_in, g_out_a, g_out_b, sink, w_o, g_ffn, w_group,
                     b_group, w_router, b_router, w1, w3, w2, g_ple, w_ple_gate, b_ple_gate, w_ple_proj, g_final)
    return (y_prompt, y_sample)
```

```python
import functools

import jax
import jax.numpy as jnp
import numpy as np
from jax import lax
from jax.experimental import pallas as pl
from jax.experimental.pallas import tpu as pltpu

HEAD_DIM = 128
A_HEADS = 8
A_WIDTH = A_HEADS * HEAD_DIM
B_HEADS = 8
B_KV_HEADS = 2
B_GROUP = B_HEADS // B_KV_HEADS
B_WIDTH = B_HEADS * HEAD_DIM
B_KV_WIDTH = B_KV_HEADS * HEAD_DIM
DILATED_CONFIGS = ((128, 1), (512, 4), (2048, 16))
B_RADIUS = 128
ROPE_THETA = 500000.0
ROPE_DIM = HEAD_DIM // 4
N_GROUPS = 4
EXPERTS_PER_GROUP = 8
N_EXPERTS = N_GROUPS * EXPERTS_PER_GROUP
RMS_EPS = 1e-6
NEG_INF = -1e30

LANES = 128
VMEM_LIMIT = 56 * 1024 * 1024

F32 = jnp.float32
BF16 = jnp.bfloat16


def _resident(shape):
    nd = len(shape)
    return pl.BlockSpec(shape, lambda *_: (0,) * nd, pipeline_mode=pl.Buffered(1))


def _params(sem):
    return pltpu.CompilerParams(dimension_semantics=sem, vmem_limit_bytes=VMEM_LIMIT)


def _rms(x, g):
    return x * lax.rsqrt(jnp.mean(x * x, axis=-1, keepdims=True) + RMS_EPS) * g


def _qkv_kernel(x_ref, g_ref, w_ref, cos_ref, sa_ref, sb_ref,
                qa_ref, ka_ref, va_ref, qb_ref, kb_ref, vb_ref):
    u = _rms(x_ref[...], g_ref[...]).astype(BF16)
    cos = cos_ref[...]
    sa = sa_ref[...]
    sb = sb_ref[...]
    half = ROPE_DIM // 2
    scale = HEAD_DIM ** -0.5

    def rope(t):
        return t * cos + pltpu.roll(t, half, 1) * sa + pltpu.roll(t, LANES - half, 1) * sb

    sections = (
        (qa_ref, 0, A_HEADS, True, scale),
        (ka_ref, A_WIDTH, A_HEADS, True, None),
        (va_ref, 2 * A_WIDTH, A_HEADS, False, None),
        (qb_ref, 3 * A_WIDTH, B_HEADS, True, scale),
        (kb_ref, 3 * A_WIDTH + B_WIDTH, B_KV_HEADS, True, None),
        (vb_ref, 3 * A_WIDTH + B_WIDTH + B_KV_WIDTH, B_KV_HEADS, False, None),
    )
    for out_ref, c0, heads, rotary, sc in sections:
        for h0 in range(0, heads, 4):
            nh = min(4, heads - h0)
            lo = c0 + h0 * HEAD_DIM
            acc = jnp.dot(u, w_ref[:, lo:lo + nh * HEAD_DIM], preferred_element_type=F32)
            for j in range(nh):
                t = acc[:, j * HEAD_DIM:(j + 1) * HEAD_DIM]
                if rotary:
                    t = rope(t)
                if sc is not None:
                    t = t * sc
                out_ref[:, (h0 + j) * HEAD_DIM:(h0 + j + 1) * HEAD_DIM] = t.astype(BF16)


def _qkv(x2d, g, w_in, cos, sa, sb, seq, tm):
    T, D = x2d.shape
    n_pos = seq // tm
    row = lambda i: (i, 0)
    tab = pl.BlockSpec((tm, LANES), lambda i: (i % n_pos, 0))
    widths = (A_WIDTH, A_WIDTH, A_WIDTH, B_WIDTH, B_KV_WIDTH, B_KV_WIDTH)
    return pl.pallas_call(
        _qkv_kernel,
        grid=(T // tm,),
        in_specs=[pl.BlockSpec((tm, D), row), _resident((1, D)), _resident(w_in.shape), tab, tab, tab],
        out_specs=[pl.BlockSpec((tm, w), row) for w in widths],
        out_shape=[jax.ShapeDtypeStruct((T, w), BF16) for w in widths],
        compiler_params=_params(("parallel",)),
        name="qkv",
    )(x2d, g, w_in, cos, sa, sb)


def _band_mask(i, lq, radius, length):
    shape = (lq, lq + 2 * radius)
    row = lax.broadcasted_iota(jnp.int32, shape, 0)
    col = lax.broadcasted_iota(jnp.int32, shape, 1)
    kpos = i * lq - radius + col
    return (jnp.abs(col - radius - row) <= radius) & (kpos >= 0) & (kpos < length)


def _window(prev_ref, cur_ref, next_ref, sl):
    return jnp.concatenate([prev_ref[:, sl], cur_ref[:, sl], next_ref[:, sl]], axis=0)


_NT = (((1,), (1,)), ((), ()))


def _attn_a_kernel(q_ref, kp_ref, kc_ref, kn_ref, vp_ref, vc_ref, vn_ref, o_ref, lse_ref,
                   *, length, lq, radius):
    valid = _band_mask(pl.program_id(2), lq, radius, length)
    lane = lax.broadcasted_iota(jnp.int32, (lq, LANES), 1)
    lse_all = jnp.zeros((lq, LANES), F32)
    for h in range(A_HEADS):
        sl = slice(h * HEAD_DIM, (h + 1) * HEAD_DIM)
        k = _window(kp_ref, kc_ref, kn_ref, sl)
        v = _window(vp_ref, vc_ref, vn_ref, sl)
        s = lax.dot_general(q_ref[:, sl], k, _NT, preferred_element_type=F32)
        s = jnp.where(valid, s, NEG_INF)
        m = jnp.max(s, axis=-1, keepdims=True)
        p = jnp.exp(s - m)
        l = jnp.sum(p, axis=-1, keepdims=True)
        o = jnp.dot(p.astype(BF16), v, preferred_element_type=F32)
        o_ref[:, sl] = (o / l).astype(o_ref.dtype)
        lse_all = jnp.where(lane == h, m + jnp.log(l), lse_all)
    lse_ref[...] = lse_all


def _attn_a(qa, ka, va, batch, seq, dil, radius, lq):
    ls = seq // dil
    lq = min(lq, ls)
    per = lq // radius
    n_halo = ls // radius
    view = lambda t: t.reshape(batch, ls, dil * A_WIDTH)
    cur = pl.BlockSpec((None, lq, A_WIDTH), lambda b, r, i: (b, i, r))
    prv = pl.BlockSpec((None, radius, A_WIDTH), lambda b, r, i: (b, jnp.maximum(i * per - 1, 0), r))
    nxt = pl.BlockSpec((None, radius, A_WIDTH),
                       lambda b, r, i: (b, jnp.minimum((i + 1) * per, n_halo - 1), r))
    o, lse = pl.pallas_call(
        functools.partial(_attn_a_kernel, length=ls, lq=lq, radius=radius),
        grid=(batch, dil, ls // lq),
        in_specs=[cur, prv, cur, nxt, prv, cur, nxt],
        out_specs=[cur, pl.BlockSpec((None, lq, LANES), lambda b, r, i: (b, i, r))],
        out_shape=[jax.ShapeDtypeStruct((batch, ls, dil * A_WIDTH), BF16),
                   jax.ShapeDtypeStruct((batch, ls, dil * LANES), F32)],
        compiler_params=_params(("parallel", "parallel", "parallel")),
        name=f"attn_a_d{dil}",
    )(view(qa), view(ka), view(ka), view(ka), view(va), view(va), view(va))
    return o.reshape(batch * seq, A_WIDTH), lse.reshape(batch * seq, LANES)


def _attn_b_kernel(sink_ref, q_ref, kp_ref, kc_ref, kn_ref, vp_ref, vc_ref, vn_ref, g_ref, o_ref,
                   *, length, lq, radius):
    valid = _band_mask(pl.program_id(1), lq, radius, length)
    outs = []
    for kv in range(B_KV_HEADS):
        sl = slice(kv * HEAD_DIM, (kv + 1) * HEAD_DIM)
        k = _window(kp_ref, kc_ref, kn_ref, sl)
        v = _window(vp_ref, vc_ref, vn_ref, sl)
        for j in range(B_GROUP):
            h = kv * B_GROUP + j
            s = lax.dot_general(q_ref[:, h * HEAD_DIM:(h + 1) * HEAD_DIM], k, _NT,
                                preferred_element_type=F32)
            s = jnp.where(valid, s, NEG_INF)
            sink = sink_ref[h]
            m = jnp.maximum(jnp.max(s, axis=-1, keepdims=True), sink)
            p = jnp.exp(s - m)
            l = jnp.sum(p, axis=-1, keepdims=True) + jnp.exp(sink - m)
            outs.append(jnp.dot(p.astype(BF16), v, preferred_element_type=F32) / l)
    o_ref[...] = _rms(jnp.concatenate(outs, axis=-1), g_ref[...]).astype(o_ref.dtype)


def _attn_b(qb, kb, vb, sink, g_out_b, batch, seq, lq):
    radius = B_RADIUS
    n_blk = seq // lq
    q3 = qb.reshape(batch, seq, B_WIDTH)
    k3 = kb.reshape(batch, seq, B_KV_WIDTH)
    v3 = vb.reshape(batch, seq, B_KV_WIDTH)
    qspec = pl.BlockSpec((None, lq, B_WIDTH), lambda b, i: (b, i, 0))
    cur = pl.BlockSpec((None, lq, B_KV_WIDTH), lambda b, i: (b, i, 0))
    prv = pl.BlockSpec((None, lq, B_KV_WIDTH), lambda b, i: (b, jnp.maximum(i - 1, 0), 0))
    nxt = pl.BlockSpec((None, lq, B_KV_WIDTH), lambda b, i: (b, jnp.minimum(i + 1, n_blk - 1), 0))
    out = pl.pallas_call(
        functools.partial(_attn_b_kernel, length=seq, lq=lq, radius=radius),
        grid=(batch, n_blk),
        in_specs=[pl.BlockSpec(memory_space=pltpu.SMEM), qspec, prv, cur, nxt, prv, cur, nxt,
                  _resident((1, B_WIDTH))],
        out_specs=qspec,
        out_shape=jax.ShapeDtypeStruct((batch, seq, B_WIDTH), BF16),
        compiler_params=_params(("parallel", "parallel")),
        name="attn_b",
    )(sink, q3, k3, k3, k3, v3, v3, v3, g_out_b)
    return out.reshape(batch * seq, B_WIDTH)


def _oproj_kernel(o1_ref, o2_ref, o3_ref, l1_ref, l2_ref, l3_ref, mb_ref, x_ref, wo_ref,
                  ga_ref, gf_ref, wr_ref, br_ref, h_ref, u_ref, route_ref):
    tm = x_ref.shape[0]
    l1, l2, l3 = l1_ref[...], l2_ref[...], l3_ref[...]
    mx = jnp.maximum(jnp.maximum(l1, l2), l3)
    e1, e2, e3 = jnp.exp(l1 - mx), jnp.exp(l2 - mx), jnp.exp(l3 - mx)
    inv = 1.0 / (e1 + e2 + e3)
    w1, w2, w3 = e1 * inv, e2 * inv, e3 * inv
    parts = []
    for h in range(A_HEADS):
        sl = slice(h * HEAD_DIM, (h + 1) * HEAD_DIM)
        bc = lambda w: jnp.broadcast_to(w[:, h:h + 1], (tm, HEAD_DIM))
        parts.append(bc(w1) * o1_ref[:, sl].astype(F32) + bc(w2) * o2_ref[:, sl].astype(F32)
                     + bc(w3) * o3_ref[:, sl].astype(F32))
    mixed_a = _rms(jnp.concatenate(parts, axis=-1), ga_ref[...]).astype(BF16)
    h = (x_ref[...]
         + jnp.dot(mixed_a, wo_ref[:A_WIDTH, :], preferred_element_type=F32)
         + jnp.dot(mb_ref[...], wo_ref[A_WIDTH:, :], preferred_element_type=F32))
    h_ref[...] = h
    u = _rms(h, gf_ref[...])
    u_ref[...] = u.astype(BF16)

    logits = jnp.dot(u, wr_ref[...], preferred_element_type=F32,
                     precision=lax.Precision.HIGHEST) + br_ref[...]
    lane = lax.broadcasted_iota(jnp.int32, (tm, LANES), 1)
    big = jnp.int32(LANES)
    is_group = (lane >= N_EXPERTS) & (lane < N_EXPERTS + N_GROUPS)
    gmax = jnp.max(jnp.where(is_group, logits, -jnp.inf), axis=-1, keepdims=True)
    gidx = jnp.min(jnp.where(is_group & (logits == gmax), lane - N_EXPERTS, big), axis=-1, keepdims=True)
    ggate = 1.0 / jnp.sum(jnp.where(is_group, jnp.exp(logits - gmax), 0.0), axis=-1, keepdims=True)
    in_group = (lane >= gidx * EXPERTS_PER_GROUP) & (lane < (gidx + 1) * EXPERTS_PER_GROUP)
    v1 = jnp.max(jnp.where(in_group, logits, -jnp.inf), axis=-1, keepdims=True)
    i1 = jnp.min(jnp.where(in_group & (logits == v1), lane, big), axis=-1, keepdims=True)
    rest = in_group & (lane != i1)
    v2 = jnp.max(jnp.where(rest, logits, -jnp.inf), axis=-1, keepdims=True)
    i2 = jnp.min(jnp.where(rest & (logits == v2), lane, big), axis=-1, keepdims=True)
    t = jnp.exp(v2 - v1)
    tw1 = ggate / (1.0 + t)
    tw2 = ggate * t / (1.0 + t)
    route_ref[...] = jnp.where(lane == i1, tw1, 0.0) + jnp.where(lane == i2, tw2, 0.0)


def _oproj(o_list, lse_list, mixed_b, x2d, w_o, g_out_a, g_ffn, w_route, b_route, tm):
    T, D = x2d.shape
    row = lambda i: (i, 0)
    aspec = pl.BlockSpec((tm, A_WIDTH), row)
    lspec = pl.BlockSpec((tm, LANES), row)
    dspec = pl.BlockSpec((tm, D), row)
    return pl.pallas_call(
        _oproj_kernel,
        grid=(T // tm,),
        in_specs=[aspec, aspec, aspec, lspec, lspec, lspec, pl.BlockSpec((tm, B_WIDTH), row), dspec,
                  _resident(w_o.shape), _resident((1, A_WIDTH)), _resident((1, D)),
                  _resident(w_route.shape), _resident((1, LANES))],
        out_specs=[dspec, dspec, lspec],
        out_shape=[jax.ShapeDtypeStruct((T, D), F32), jax.ShapeDtypeStruct((T, D), BF16),
                   jax.ShapeDtypeStruct((T, LANES), F32)],
        compiler_params=_params(("parallel",)),
        name="oproj_route",
    )(*o_list, *lse_list, mixed_b, x2d, w_o, g_out_a, g_ffn, w_route, b_route)


def _moe_kernel(u_ref, comb_ref, h_ref, w1_ref, w3_ref, w2_ref, o_ref):
    e = pl.program_id(1)

    @pl.when(e == 0)
    def _():
        o_ref[...] = h_ref[...]

    u = u_ref[...]
    a = jnp.dot(u, w1_ref[...], preferred_element_type=F32)
    b = jnp.dot(u, w3_ref[...], preferred_element_type=F32)
    lane = lax.broadcasted_iota(jnp.int32, comb_ref.shape, 1)
    c = jnp.sum(jnp.where(lane == e, comb_ref[...], 0.0), axis=-1, keepdims=True)
    hid = (a * jax.nn.sigmoid(a)) * b * c
    o_ref[...] += jnp.dot(hid.astype(BF16), w2_ref[...], preferred_element_type=F32)


def _moe(u, comb, h, w1, w3, w2, tm):
    T, D = h.shape
    E, _, F = w1.shape
    row = lambda i, e: (i, 0)
    return pl.pallas_call(
        _moe_kernel,
        grid=(T // tm, E),
        in_specs=[pl.BlockSpec((tm, D), row), pl.BlockSpec((tm, LANES), row), pl.BlockSpec((tm, D), row),
                  pl.BlockSpec((None, D, F), lambda i, e: (e, 0, 0)),
                  pl.BlockSpec((None, D, F), lambda i, e: (e, 0, 0)),
                  pl.BlockSpec((None, F, D), lambda i, e: (e, 0, 0))],
        out_specs=pl.BlockSpec((tm, D), row),
        out_shape=jax.ShapeDtypeStruct((T, D), F32),
        compiler_params=_params(("parallel", "arbitrary")),
        name="moe_dense",
    )(u, comb, h, w1, w3, w2)


def _ple_kernel(h_ref, p_ref, gp_ref, wg_ref, bg_ref, wp_ref, gf_ref, o_ref):
    h = h_ref[...]
    u = _rms(h, gp_ref[...]).astype(BF16)
    gate = jax.nn.sigmoid(jnp.dot(u, wg_ref[...], preferred_element_type=F32) + bg_ref[...])
    proj = jnp.dot(p_ref[...].astype(BF16), wp_ref[...], preferred_element_type=F32)
    o_ref[...] = _rms(h + gate * proj, gf_ref[...])


def _ple(h, p2d, g_ple, w_gate, b_gate, w_proj, g_final, tm):
    T, D = h.shape
    P = p2d.shape[1]
    row = lambda i: (i, 0)
    return pl.pallas_call(
        _ple_kernel,
        grid=(T // tm,),
        in_specs=[pl.BlockSpec((tm, D), row), pl.BlockSpec((tm, P), row), _resident((1, D)),
                  _resident(w_gate.shape), _resident((1, D)), _resident(w_proj.shape), _resident((1, D))],
        out_specs=pl.BlockSpec((tm, D), row),
        out_shape=jax.ShapeDtypeStruct((T, D), F32),
        compiler_params=_params(("parallel",)),
        name="ple_final",
    )(h, p2d, g_ple, w_gate, b_gate, w_proj, g_final)


def _rope_tables(seq):
    half = ROPE_DIM // 2
    inv = ROPE_THETA ** (-(jnp.arange(half, dtype=F32) * 2.0) / ROPE_DIM)
    ang = jnp.arange(seq, dtype=F32)[:, None] * inv[None, :]
    cos, sin = jnp.cos(ang), jnp.sin(ang)
    zeros = jnp.zeros((seq, LANES - ROPE_DIM), F32)
    z16 = jnp.zeros((seq, half), F32)
    cos_t = jnp.concatenate([cos, cos, jnp.ones((seq, LANES - ROPE_DIM), F32)], axis=-1)
    sa_t = jnp.concatenate([z16, sin, zeros], axis=-1)
    sb_t = jnp.concatenate([-sin, z16, zeros], axis=-1)
    return cos_t, sa_t, sb_t


def _trunk(x, p, wts):
    batch, seq, D = x.shape
    T = batch * seq
    x2d = x.reshape(T, D)
    cos, sa, sb = _rope_tables(seq)
    qa, ka, va, qb, kb, vb = _qkv(x2d, wts["g_attn"], wts["w_in"], cos, sa, sb, seq, tm=256)
    o_list, lse_list = [], []
    for window, dil in DILATED_CONFIGS:
        o, lse = _attn_a(qa, ka, va, batch, seq, dil, window // (2 * dil), lq=128)
        o_list.append(o)
        lse_list.append(lse)
    mixed_b = _attn_b(qb, kb, vb, wts["sink"], wts["g_out_b"], batch, seq, lq=128)
    h1, u2, comb = _oproj(o_list, lse_list, mixed_b, x2d, wts["w_o"], wts["g_out_a"], wts["g_ffn"],
                          wts["w_route"], wts["b_route"], tm=256)
    h2 = _moe(u2, comb, h1, wts["w1"], wts["w3"], wts["w2"], tm=512)
    y = _ple(h2, p.reshape(T, -1), wts["g_ple"], wts["w_ple_gate"], wts["b_ple_gate"],
             wts["w_ple_proj"], wts["g_final"], tm=256)
    return y.reshape(batch, seq, D)


def kernel(x_prompt, x_sample, p_prompt, p_sample, g_attn, w_in, g_out_a, g_out_b, sink, w_o, g_ffn, w_group, b_group, w_router, b_router, w1, w3, w2, g_ple, w_ple_gate, b_ple_gate, w_ple_proj, g_final):
    depth = w_in.shape[0]
    assert depth == 1
    l = 0
    D = w_in.shape[1]
    pad = LANES - N_EXPERTS - N_GROUPS
    wts = {
        "g_attn": g_attn[l].reshape(1, D),
        "w_in": w_in[l].astype(BF16),
        "g_out_a": g_out_a[l].reshape(1, A_WIDTH),
        "g_out_b": g_out_b[l].reshape(1, B_WIDTH),
        "sink": sink[l],
        "w_o": w_o[l].astype(BF16),
        "g_ffn": g_ffn[l].reshape(1, D),
        "w_route": jnp.pad(jnp.concatenate([w_router[l], w_group[l]], axis=-1), ((0, 0), (0, pad))),
        "b_route": jnp.pad(jnp.concatenate([b_router[l], b_group[l]]), (0, pad)).reshape(1, LANES),
        "w1": w1[l].astype(BF16),
        "w3": w3[l].astype(BF16),
        "w2": w2[l].astype(BF16),
        "g_ple": g_ple[l].reshape(1, D),
        "w_ple_gate": w_ple_gate[l].astype(BF16),
        "b_ple_gate": b_ple_gate[l].reshape(1, D),
        "w_ple_proj": w_ple_proj[l].astype(BF16),
        "g_final": g_final.reshape(1, D),
    }
    return (_trunk(x_prompt, p_prompt[l], wts), _trunk(x_sample, p_sample[l], wts))
```

```python
import functools

import jax
import jax.numpy as jnp
import numpy as np
from jax import lax
from jax.experimental import pallas as pl
from jax.experimental.pallas import tpu as pltpu

HEAD_DIM = 128
A_HEADS = 8
A_WIDTH = A_HEADS * HEAD_DIM
B_HEADS = 8
B_KV_HEADS = 2
B_GROUP = B_HEADS // B_KV_HEADS
B_WIDTH = B_HEADS * HEAD_DIM
B_KV_WIDTH = B_KV_HEADS * HEAD_DIM
DILATED_CONFIGS = ((128, 1), (512, 4), (2048, 16))
B_RADIUS = 128
ROPE_THETA = 500000.0
ROPE_DIM = HEAD_DIM // 4
N_GROUPS = 4
EXPERTS_PER_GROUP = 8
N_EXPERTS = N_GROUPS * EXPERTS_PER_GROUP
RMS_EPS = 1e-6
NEG_INF = -1e30

LANES = 128
VMEM_LIMIT = 56 * 1024 * 1024

F32 = jnp.float32
BF16 = jnp.bfloat16


def _resident(shape):
    nd = len(shape)
    return pl.BlockSpec(shape, lambda *_: (0,) * nd, pipeline_mode=pl.Buffered(1))


def _params(sem):
    return pltpu.CompilerParams(dimension_semantics=sem, vmem_limit_bytes=VMEM_LIMIT)


def _rms(x, g):
    return x * lax.rsqrt(jnp.mean(x * x, axis=-1, keepdims=True) + RMS_EPS) * g


def _qkv_kernel(x_ref, g_ref, w_ref, cos_ref, sa_ref, sb_ref,
                qa_ref, ka_ref, va_ref, qb_ref, kb_ref, vb_ref):
    u = _rms(x_ref[...], g_ref[...]).astype(BF16)
    cos = cos_ref[...]
    sa = sa_ref[...]
    sb = sb_ref[...]
    half = ROPE_DIM // 2
    scale = HEAD_DIM ** -0.5

    def rope(t):
        return t * cos + pltpu.roll(t, half, 1) * sa + pltpu.roll(t, LANES - half, 1) * sb

    sections = (
        (qa_ref, 0, A_HEADS, True, scale),
        (ka_ref, A_WIDTH, A_HEADS, True, None),
        (va_ref, 2 * A_WIDTH, A_HEADS, False, None),
        (qb_ref, 3 * A_WIDTH, B_HEADS, True, scale),
        (kb_ref, 3 * A_WIDTH + B_WIDTH, B_KV_HEADS, True, None),
        (vb_ref, 3 * A_WIDTH + B_WIDTH + B_KV_WIDTH, B_KV_HEADS, False, None),
    )
    for out_ref, c0, heads, rotary, sc in sections:
        for h0 in range(0, heads, 4):
            nh = min(4, heads - h0)
            lo = c0 + h0 * HEAD_DIM
            acc = jnp.dot(u, w_ref[:, lo:lo + nh * HEAD_DIM], preferred_element_type=F32)
            for j in range(nh):
                t = acc[:, j * HEAD_DIM:(j + 1) * HEAD_DIM]
                if rotary:
                    t = rope(t)
                if sc is not None:
                    t = t * sc
                out_ref[:, (h0 + j) * HEAD_DIM:(h0 + j + 1) * HEAD_DIM] = t.astype(BF16)


def _qkv(x2d, g, w_in, cos, sa, sb, seq, tm):
    T, D = x2d.shape
    n_pos = seq // tm
    row = lambda i: (i, 0)
    tab = pl.BlockSpec((tm, LANES), lambda i: (i % n_pos, 0))
    widths = (A_WIDTH, A_WIDTH, A_WIDTH, B_WIDTH, B_KV_WIDTH, B_KV_WIDTH)
    return pl.pallas_call(
        _qkv_kernel,
        grid=(T // tm,),
        in_specs=[pl.BlockSpec((tm, D), row), _resident((1, D)), _resident(w_in.shape), tab, tab, tab],
        out_specs=[pl.BlockSpec((tm, w), row) for w in widths],
        out_shape=[jax.ShapeDtypeStruct((T, w), BF16) for w in widths],
        compiler_params=_params(("parallel",)),
        name="qkv",
    )(x2d, g, w_in, cos, sa, sb)


def _band_mask(i, lq, radius, length):
    shape = (lq, lq + 2 * radius)
    row = lax.broadcasted_iota(jnp.int32, shape, 0)
    col = lax.broadcasted_iota(jnp.int32, shape, 1)
    kpos = i * lq - radius + col
    return (jnp.abs(col - radius - row) <= radius) & (kpos >= 0) & (kpos < length)


def _window(prev_ref, cur_ref, next_ref, sl):
    return jnp.concatenate([prev_ref[:, sl], cur_ref[:, sl], next_ref[:, sl]], axis=0)


_NT = (((1,), (1,)), ((), ()))


def _attn_a_kernel(q_ref, kp_ref, kc_ref, kn_ref, vp_ref, vc_ref, vn_ref, o_ref, lse_ref,
                   *, length, lq, radius):
    valid = _band_mask(pl.program_id(2), lq, radius, length)
    lane = lax.broadcasted_iota(jnp.int32, (lq, LANES), 1)
    lse_all = jnp.zeros((lq, LANES), F32)
    for h in range(A_HEADS):
        sl = slice(h * HEAD_DIM, (h + 1) * HEAD_DIM)
        k = _window(kp_ref, kc_ref, kn_ref, sl)
        v = _window(vp_ref, vc_ref, vn_ref, sl)
        s = lax.dot_general(q_ref[:, sl], k, _NT, preferred_element_type=F32)
        s = jnp.where(valid, s, NEG_INF)
        m = jnp.max(s, axis=-1, keepdims=True)
        p = jnp.exp(s - m)
        l = jnp.sum(p, axis=-1, keepdims=True)
        o = jnp.dot(p.astype(BF16), v, preferred_element_type=F32)
        o_ref[:, sl] = (o / l).astype(o_ref.dtype)
        lse_all = jnp.where(lane == h, m + jnp.log(l), lse_all)
    lse_ref[...] = lse_all


def _attn_a(qa, ka, va, batch, seq, dil, radius, lq):
    ls = seq // dil
    lq = min(lq, ls)
    per = lq // radius
    n_halo = ls // radius
    view = lambda t: t.reshape(batch, ls, dil * A_WIDTH)
    cur = pl.BlockSpec((None, lq, A_WIDTH), lambda b, r, i: (b, i, r))
    prv = pl.BlockSpec((None, radius, A_WIDTH), lambda b, r, i: (b, jnp.maximum(i * per - 1, 0), r))
    nxt = pl.BlockSpec((None, radius, A_WIDTH),
                       lambda b, r, i: (b, jnp.minimum((i + 1) * per, n_halo - 1), r))
    o, lse = pl.pallas_call(
        functools.partial(_attn_a_kernel, length=ls, lq=lq, radius=radius),
        grid=(batch, dil, ls // lq),
        in_specs=[cur, prv, cur, nxt, prv, cur, nxt],
        out_specs=[cur, pl.BlockSpec((None, lq, LANES), lambda b, r, i: (b, i, r))],
        out_shape=[jax.ShapeDtypeStruct((batch, ls, dil * A_WIDTH), BF16),
                   jax.ShapeDtypeStruct((batch, ls, dil * LANES), F32)],
        compiler_params=_params(("parallel", "parallel", "parallel")),
        name=f"attn_a_d{dil}",
    )(view(qa), view(ka), view(ka), view(ka), view(va), view(va), view(va))
    return o.reshape(batch * seq, A_WIDTH), lse.reshape(batch * seq, LANES)


def _attn_b_kernel(sink_ref, q_ref, kp_ref, kc_ref, kn_ref, vp_ref, vc_ref, vn_ref, g_ref, o_ref,
                   *, length, lq, radius):
    valid = _band_mask(pl.program_id(1), lq, radius, length)
    outs = []
    for kv in range(B_KV_HEADS):
        sl = slice(kv * HEAD_DIM, (kv + 1) * HEAD_DIM)
        k = _window(kp_ref, kc_ref, kn_ref, sl)
        v = _window(vp_ref, vc_ref, vn_ref, sl)
        for j in range(B_GROUP):
            h = kv * B_GROUP + j
            s = lax.dot_general(q_ref[:, h * HEAD_DIM:(h + 1) * HEAD_DIM], k, _NT,
                                preferred_element_type=F32)
            s = jnp.where(valid, s, NEG_INF)
            sink = sink_ref[h]
            m = jnp.maximum(jnp.max(s, axis=-1, keepdims=True), sink)
            p = jnp.exp(s - m)
            l = jnp.sum(p, axis=-1, keepdims=True) + jnp.exp(sink - m)
            outs.append(jnp.dot(p.astype(BF16), v, preferred_element_type=F32) / l)
    o_ref[...] = _rms(jnp.concatenate(outs, axis=-1), g_ref[...]).astype(o_ref.dtype)


def _attn_b(qb, kb, vb, sink, g_out_b, batch, seq, lq):
    radius = B_RADIUS
    n_blk = seq // lq
    q3 = qb.reshape(batch, seq, B_WIDTH)
    k3 = kb.reshape(batch, seq, B_KV_WIDTH)
    v3 = vb.reshape(batch, seq, B_KV_WIDTH)
    qspec = pl.BlockSpec((None, lq, B_WIDTH), lambda b, i: (b, i, 0))
    cur = pl.BlockSpec((None, lq, B_KV_WIDTH), lambda b, i: (b, i, 0))
    prv = pl.BlockSpec((None, lq, B_KV_WIDTH), lambda b, i: (b, jnp.maximum(i - 1, 0), 0))
    nxt = pl.BlockSpec((None, lq, B_KV_WIDTH), lambda b, i: (b, jnp.minimum(i + 1, n_blk - 1), 0))
    out = pl.pallas_call(
        functools.partial(_attn_b_kernel, length=seq, lq=lq, radius=radius),
        grid=(batch, n_blk),
        in_specs=[pl.BlockSpec(memory_space=pltpu.SMEM), qspec, prv, cur, nxt, prv, cur, nxt,
                  _resident((1, B_WIDTH))],
        out_specs=qspec,
        out_shape=jax.ShapeDtypeStruct((batch, seq, B_WIDTH), BF16),
        compiler_params=_params(("parallel", "parallel")),
        name="attn_b",
    )(sink, q3, k3, k3, k3, v3, v3, v3, g_out_b)
    return out.reshape(batch * seq, B_WIDTH)


R_E1, R_E2, R_W1, R_W2, R_RANK1, R_RANK2 = range(6)


def _oproj_kernel(o1_ref, o2_ref, o3_ref, l1_ref, l2_ref, l3_ref, mb_ref, x_ref, wo_ref,
                  ga_ref, gf_ref, wr_ref, br_ref, cnt_in_ref, h_ref, u_ref, route_ref, cnt_ref):
    tm = x_ref.shape[0]

    @pl.when(pl.program_id(0) == 0)
    def _():
        cnt_ref[...] = cnt_in_ref[...]

    l1, l2, l3 = l1_ref[...], l2_ref[...], l3_ref[...]
    mx = jnp.maximum(jnp.maximum(l1, l2), l3)
    e1, e2, e3 = jnp.exp(l1 - mx), jnp.exp(l2 - mx), jnp.exp(l3 - mx)
    inv = 1.0 / (e1 + e2 + e3)
    w1, w2, w3 = e1 * inv, e2 * inv, e3 * inv
    parts = []
    for h in range(A_HEADS):
        sl = slice(h * HEAD_DIM, (h + 1) * HEAD_DIM)
        bc = lambda w: jnp.broadcast_to(w[:, h:h + 1], (tm, HEAD_DIM))
        parts.append(bc(w1) * o1_ref[:, sl].astype(F32) + bc(w2) * o2_ref[:, sl].astype(F32)
                     + bc(w3) * o3_ref[:, sl].astype(F32))
    mixed_a = _rms(jnp.concatenate(parts, axis=-1), ga_ref[...]).astype(BF16)
    h = (x_ref[...]
         + jnp.dot(mixed_a, wo_ref[:A_WIDTH, :], preferred_element_type=F32)
         + jnp.dot(mb_ref[...], wo_ref[A_WIDTH:, :], preferred_element_type=F32))
    h_ref[...] = h
    u = _rms(h, gf_ref[...])
    u_ref[...] = u

    logits = jnp.dot(u, wr_ref[...], preferred_element_type=F32,
                     precision=lax.Precision.HIGHEST) + br_ref[...]
    lane = lax.broadcasted_iota(jnp.int32, (tm, LANES), 1)
    big = jnp.int32(LANES)
    is_group = (lane >= N_EXPERTS) & (lane < N_EXPERTS + N_GROUPS)
    gmax = jnp.max(jnp.where(is_group, logits, -jnp.inf), axis=-1, keepdims=True)
    gidx = jnp.min(jnp.where(is_group & (logits == gmax), lane - N_EXPERTS, big), axis=-1, keepdims=True)
    ggate = 1.0 / jnp.sum(jnp.where(is_group, jnp.exp(logits - gmax), 0.0), axis=-1, keepdims=True)
    in_group = (lane >= gidx * EXPERTS_PER_GROUP) & (lane < (gidx + 1) * EXPERTS_PER_GROUP)
    v1 = jnp.max(jnp.where(in_group, logits, -jnp.inf), axis=-1, keepdims=True)
    i1 = jnp.min(jnp.where(in_group & (logits == v1), lane, big), axis=-1, keepdims=True)
    rest = in_group & (lane != i1)
    v2 = jnp.max(jnp.where(rest, logits, -jnp.inf), axis=-1, keepdims=True)
    i2 = jnp.min(jnp.where(rest & (logits == v2), lane, big), axis=-1, keepdims=True)
    t = jnp.exp(v2 - v1)
    tw1 = ggate / (1.0 + t)
    tw2 = ggate * t / (1.0 + t)

    onehot = ((lane == i1) | (lane == i2)).astype(F32)
    r_i = lax.broadcasted_iota(jnp.int32, (tm, tm), 0)
    c_i = lax.broadcasted_iota(jnp.int32, (tm, tm), 1)
    before = (c_i < r_i).astype(BF16)
    prefix = jnp.dot(before, onehot.astype(BF16), preferred_element_type=F32) + cnt_ref[...]
    rank1 = jnp.sum(jnp.where(lane == i1, prefix, 0.0), axis=-1, keepdims=True)
    rank2 = jnp.sum(jnp.where(lane == i2, prefix, 0.0), axis=-1, keepdims=True)
    cnt_ref[...] += jnp.sum(onehot, axis=0, keepdims=True)

    rec = jnp.zeros((tm, LANES), F32)
    for slot, val in ((R_E1, i1.astype(F32)), (R_E2, i2.astype(F32)), (R_W1, tw1), (R_W2, tw2),
                      (R_RANK1, rank1), (R_RANK2, rank2)):
        rec = jnp.where(lane == slot, val, rec)
    route_ref[...] = rec


def _oproj(o_list, lse_list, mixed_b, x2d, w_o, g_out_a, g_ffn, w_route, b_route, cnt_in, tm):
    T, D = x2d.shape
    row = lambda i: (i, 0)
    aspec = pl.BlockSpec((tm, A_WIDTH), row)
    lspec = pl.BlockSpec((tm, LANES), row)
    dspec = pl.BlockSpec((tm, D), row)
    cspec = pl.BlockSpec((1, LANES), lambda i: (0, 0))
    return pl.pallas_call(
        _oproj_kernel,
        grid=(T // tm,),
        in_specs=[aspec, aspec, aspec, lspec, lspec, lspec, pl.BlockSpec((tm, B_WIDTH), row), dspec,
                  _resident(w_o.shape), _resident((1, A_WIDTH)), _resident((1, D)),
                  _resident(w_route.shape), _resident((1, LANES)), cspec],
        out_specs=[dspec, dspec, lspec, cspec],
        out_shape=[jax.ShapeDtypeStruct((T, D), F32), jax.ShapeDtypeStruct((T, D), F32),
                   jax.ShapeDtypeStruct((T, LANES), F32), jax.ShapeDtypeStruct((1, LANES), F32)],
        compiler_params=_params(("arbitrary",)),
        name="oproj_route",
    )(*o_list, *lse_list, mixed_b, x2d, w_o, g_out_a, g_ffn, w_route, b_route, cnt_in)


def _row_copy(src_ref, src_row, dst_ref, dst_row, sem):
    return pltpu.make_async_copy(src_ref.at[pl.ds(src_row, 1)], dst_ref.at[pl.ds(dst_row, 1)], sem)


def _dispatch_kernel(pos_ref, u_ref, xs_in_ref, xs_ref, sem, *, tm):
    del xs_in_ref

    def issue(j, carry):
        _row_copy(u_ref, j, xs_ref, pos_ref[0, j], sem).start()
        _row_copy(u_ref, j, xs_ref, pos_ref[0, tm + j], sem).start()
        return carry

    lax.fori_loop(0, tm, issue, 0, unroll=4)

    def drain(j, carry):
        _row_copy(u_ref, 0, xs_ref, 0, sem).wait()
        return carry

    lax.fori_loop(0, 2 * tm, drain, 0, unroll=8)


def _tile_positions(pos1, pos2, tm):
    n = pos1.shape[0] // tm
    return jnp.stack([pos1.reshape(n, tm), pos2.reshape(n, tm)], axis=1).reshape(n, 1, 2 * tm)


def _dispatch(u, pos_tiles, xs, tm):
    T, D = u.shape
    return pl.pallas_call(
        functools.partial(_dispatch_kernel, tm=tm),
        grid=(T // tm,),
        in_specs=[pl.BlockSpec((None, 1, 2 * tm), lambda i: (i, 0, 0), memory_space=pltpu.SMEM),
                  pl.BlockSpec((tm, D), lambda i: (i, 0)),
                  pl.BlockSpec(memory_space=pl.ANY)],
        out_specs=pl.BlockSpec(memory_space=pl.ANY),
        out_shape=jax.ShapeDtypeStruct(xs.shape, xs.dtype),
        scratch_shapes=[pltpu.SemaphoreType.DMA(())],
        input_output_aliases={2: 0},
        compiler_params=_params(("arbitrary",)),
        name="moe_dispatch",
    )(pos_tiles, u, xs)


def _experts_kernel(te_ref, x_ref, w1_ref, w3_ref, w2_ref, y_ref):
    del te_ref
    x = x_ref[...].astype(BF16)
    a = jnp.dot(x, w1_ref[...], preferred_element_type=F32)
    b = jnp.dot(x, w3_ref[...], preferred_element_type=F32)
    hid = (a * jax.nn.sigmoid(a)) * b
    y_ref[...] = jnp.dot(hid.astype(BF16), w2_ref[...], preferred_element_type=F32)


def _experts(xs, tile_expert, w1, w3, w2, tm):
    P, D = xs.shape
    F = w1.shape[2]
    row = lambda i, te: (i, 0)
    grid_spec = pltpu.PrefetchScalarGridSpec(
        num_scalar_prefetch=1,
        grid=(P // tm,),
        in_specs=[pl.BlockSpec((tm, D), row),
                  pl.BlockSpec((None, D, F), lambda i, te: (te[i], 0, 0)),
                  pl.BlockSpec((None, D, F), lambda i, te: (te[i], 0, 0)),
                  pl.BlockSpec((None, F, D), lambda i, te: (te[i], 0, 0))],
        out_specs=pl.BlockSpec((tm, D), row),
    )
    return pl.pallas_call(
        _experts_kernel,
        grid_spec=grid_spec,
        out_shape=jax.ShapeDtypeStruct((P, D), F32),
        compiler_params=_params(("parallel",)),
        name="moe_experts",
    )(tile_expert, xs, w1, w3, w2)


def _ple_kernel(pos_ref, ys_ref, h_ref, route_ref, p_ref, gp_ref, wg_ref, bg_ref, wp_ref, gf_ref, o_ref,
                ybuf, sem, *, tm):
    def issue(j, carry):
        _row_copy(ys_ref, pos_ref[0, j], ybuf.at[0], j, sem).start()
        _row_copy(ys_ref, pos_ref[0, tm + j], ybuf.at[1], j, sem).start()
        return carry

    lax.fori_loop(0, tm, issue, 0, unroll=4)

    def drain(j, carry):
        _row_copy(ys_ref, 0, ybuf.at[0], 0, sem).wait()
        return carry

    lax.fori_loop(0, 2 * tm, drain, 0, unroll=8)

    route = route_ref[...]
    h = (h_ref[...] + route[:, R_W1:R_W1 + 1] * ybuf[0] + route[:, R_W2:R_W2 + 1] * ybuf[1])
    u = _rms(h, gp_ref[...]).astype(BF16)
    gate = jax.nn.sigmoid(jnp.dot(u, wg_ref[...], preferred_element_type=F32) + bg_ref[...])
    proj = jnp.dot(p_ref[...].astype(BF16), wp_ref[...], preferred_element_type=F32)
    o_ref[...] = _rms(h + gate * proj, gf_ref[...])


def _ple(pos_tiles, ys, h, route, p2d, g_ple, w_gate, b_gate, w_proj, g_final, tm):
    T, D = h.shape
    P = p2d.shape[1]
    row = lambda i: (i, 0)
    return pl.pallas_call(
        functools.partial(_ple_kernel, tm=tm),
        grid=(T // tm,),
        in_specs=[pl.BlockSpec((None, 1, 2 * tm), lambda i: (i, 0, 0), memory_space=pltpu.SMEM),
                  pl.BlockSpec(memory_space=pl.ANY),
                  pl.BlockSpec((tm, D), row), pl.BlockSpec((tm, LANES), row), pl.BlockSpec((tm, P), row),
                  _resident((1, D)), _resident(w_gate.shape), _resident((1, D)), _resident(w_proj.shape),
                  _resident((1, D))],
        out_specs=pl.BlockSpec((tm, D), row),
        out_shape=jax.ShapeDtypeStruct((T, D), F32),
        scratch_shapes=[pltpu.VMEM((2, tm, D), F32), pltpu.SemaphoreType.DMA(())],
        compiler_params=_params(("arbitrary",)),
        name="ple_final",
    )(pos_tiles, ys, h, route, p2d, g_ple, w_gate, b_gate, w_proj, g_final)


def _rope_tables(seq):
    half = ROPE_DIM // 2
    inv = ROPE_THETA ** (-(jnp.arange(half, dtype=F32) * 2.0) / ROPE_DIM)
    ang = jnp.arange(seq, dtype=F32)[:, None] * inv[None, :]
    cos, sin = jnp.cos(ang), jnp.sin(ang)
    zeros = jnp.zeros((seq, LANES - ROPE_DIM), F32)
    z16 = jnp.zeros((seq, half), F32)
    cos_t = jnp.concatenate([cos, cos, jnp.ones((seq, LANES - ROPE_DIM), F32)], axis=-1)
    sa_t = jnp.concatenate([z16, sin, zeros], axis=-1)
    sb_t = jnp.concatenate([-sin, z16, zeros], axis=-1)
    return cos_t, sa_t, sb_t


MOE_TILE = 256
ROUTE_TILE = 256
DISPATCH_TILE = 1024


def _mixer(x, wts, cnt_in):
    batch, seq, D = x.shape
    T = batch * seq
    x2d = x.reshape(T, D)
    cos, sa, sb = _rope_tables(seq)
    qa, ka, va, qb, kb, vb = _qkv(x2d, wts["g_attn"], wts["w_in"], cos, sa, sb, seq, tm=256)
    o_list, lse_list = [], []
    for window, dil in DILATED_CONFIGS:
        o, lse = _attn_a(qa, ka, va, batch, seq, dil, window // (2 * dil), lq=128)
        o_list.append(o)
        lse_list.append(lse)
    mixed_b = _attn_b(qb, kb, vb, wts["sink"], wts["g_out_b"], batch, seq, lq=128)
    return _oproj(o_list, lse_list, mixed_b, x2d, wts["w_o"], wts["g_out_a"], wts["g_ffn"],
                  wts["w_route"], wts["b_route"], cnt_in, tm=ROUTE_TILE)


def _sorted_layout(counts, n_tiles, tm):
    cnt = counts[0, :N_EXPERTS].astype(jnp.int32)
    tiles = (cnt + tm - 1) // tm
    ends = jnp.cumsum(tiles)
    offsets = (ends - tiles) * tm
    tile_expert = jnp.sum(jnp.arange(n_tiles, dtype=jnp.int32)[:, None] >= ends[None, :], axis=1)
    return offsets, jnp.minimum(tile_expert, N_EXPERTS - 1).astype(jnp.int32)


def _positions(route, offsets):
    e1 = route[:, R_E1].astype(jnp.int32)
    e2 = route[:, R_E2].astype(jnp.int32)
    pos1 = jnp.take(offsets, e1) + route[:, R_RANK1].astype(jnp.int32)
    pos2 = jnp.take(offsets, e2) + route[:, R_RANK2].astype(jnp.int32)
    return pos1, pos2


def kernel(x_prompt, x_sample, p_prompt, p_sample, g_attn, w_in, g_out_a, g_out_b, sink, w_o, g_ffn, w_group, b_group, w_router, b_router, w1, w3, w2, g_ple, w_ple_gate, b_ple_gate, w_ple_proj, g_final):
    depth = w_in.shape[0]
    assert depth == 1
    l = 0
    D = w_in.shape[1]
    pad = LANES - N_EXPERTS - N_GROUPS
    wts = {
        "g_attn": g_attn[l].reshape(1, D),
        "w_in": w_in[l].astype(BF16),
        "g_out_a": g_out_a[l].reshape(1, A_WIDTH),
        "g_out_b": g_out_b[l].reshape(1, B_WIDTH),
        "sink": sink[l],
        "w_o": w_o[l].astype(BF16),
        "g_ffn": g_ffn[l].reshape(1, D),
        "w_route": jnp.pad(jnp.concatenate([w_router[l], w_group[l]], axis=-1), ((0, 0), (0, pad))),
        "b_route": jnp.pad(jnp.concatenate([b_router[l], b_group[l]]), (0, pad)).reshape(1, LANES),
        "w1": w1[l].astype(BF16),
        "w3": w3[l].astype(BF16),
        "w2": w2[l].astype(BF16),
        "g_ple": g_ple[l].reshape(1, D),
        "w_ple_gate": w_ple_gate[l].astype(BF16),
        "b_ple_gate": b_ple_gate[l].reshape(1, D),
        "w_ple_proj": w_ple_proj[l].astype(BF16),
        "g_final": g_final.reshape(1, D),
    }
    xs_list = (x_prompt, x_sample)
    ps_list = (p_prompt[l], p_sample[l])

    counts = jnp.zeros((1, LANES), F32)
    mixed = []
    for x in xs_list:
        h1, u2, route, counts = _mixer(x, wts, counts)
        mixed.append((h1, u2, route))

    n_pairs = 2 * sum(x.shape[0] * x.shape[1] for x in xs_list)
    n_tiles = n_pairs // MOE_TILE + N_EXPERTS
    offsets, tile_expert = _sorted_layout(counts, n_tiles, MOE_TILE)
    xs = jnp.zeros((n_tiles * MOE_TILE, D), F32)
    pos = []
    for h1, u2, route in mixed:
        pos1, pos2 = _positions(route, offsets)
        pos.append((pos1, pos2))
        xs = _dispatch(u2, _tile_positions(pos1, pos2, DISPATCH_TILE), xs, DISPATCH_TILE)
    ys = _experts(xs, tile_expert, wts["w1"], wts["w3"], wts["w2"], MOE_TILE)

    outs = []
    for x, p, (h1, u2, route), (pos1, pos2) in zip(xs_list, ps_list, mixed, pos):
        y = _ple(_tile_positions(pos1, pos2, ROUTE_TILE), ys, h1, route, p.reshape(h1.shape[0], -1),
                 wts["g_ple"], wts["w_ple_gate"], wts["b_ple_gate"], wts["w_ple_proj"], wts["g_final"],
                 ROUTE_TILE)
        outs.append(y.reshape(x.shape))
    return tuple(outs)
```

```python
import functools

import jax
import jax.numpy as jnp
import numpy as np
from jax import lax
from jax.experimental import pallas as pl
from jax.experimental.pallas import tpu as pltpu

HEAD_DIM = 128
A_HEADS = 8
A_WIDTH = A_HEADS * HEAD_DIM
B_HEADS = 8
B_KV_HEADS = 2
B_GROUP = B_HEADS // B_KV_HEADS
B_WIDTH = B_HEADS * HEAD_DIM
B_KV_WIDTH = B_KV_HEADS * HEAD_DIM
DILATED_CONFIGS = ((128, 1), (512, 4), (2048, 16))
B_RADIUS = 128
ROPE_THETA = 500000.0
ROPE_DIM = HEAD_DIM // 4
N_GROUPS = 4
EXPERTS_PER_GROUP = 8
N_EXPERTS = N_GROUPS * EXPERTS_PER_GROUP
RMS_EPS = 1e-6
NEG_INF = -1e30

LANES = 128
VMEM_LIMIT = 56 * 1024 * 1024

F32 = jnp.float32
BF16 = jnp.bfloat16


def _resident(shape):
    nd = len(shape)
    return pl.BlockSpec(shape, lambda *_: (0,) * nd, pipeline_mode=pl.Buffered(1))


def _params(sem):
    return pltpu.CompilerParams(dimension_semantics=sem, vmem_limit_bytes=VMEM_LIMIT)


def _rms(x, g):
    return x * lax.rsqrt(jnp.mean(x * x, axis=-1, keepdims=True) + RMS_EPS) * g


STRIDED_DILATIONS = tuple(d for _, d in DILATED_CONFIGS if d > 1)


def _qkv_kernel(x_ref, g_ref, w_ref, cos_ref, sa_ref, sb_ref,
                qa_ref, ka_ref, va_ref, qb_ref, kb_ref, vb_ref, *rest):
    n_dil = len(STRIDED_DILATIONS)
    dil_refs = [rest[i * n_dil:(i + 1) * n_dil] for i in range(3)]
    stage_refs = rest[3 * n_dil:]
    tm = x_ref.shape[0]
    u = _rms(x_ref[...], g_ref[...]).astype(BF16)
    cos = cos_ref[...]
    sa = sa_ref[...]
    sb = sb_ref[...]
    half = ROPE_DIM // 2
    scale = HEAD_DIM ** -0.5

    def rope(t):
        return t * cos + pltpu.roll(t, half, 1) * sa + pltpu.roll(t, LANES - half, 1) * sb

    sections = (
        (qa_ref, 0, A_HEADS, True, scale, 0),
        (ka_ref, A_WIDTH, A_HEADS, True, None, 1),
        (va_ref, 2 * A_WIDTH, A_HEADS, False, None, 2),
        (qb_ref, 3 * A_WIDTH, B_HEADS, True, scale, None),
        (kb_ref, 3 * A_WIDTH + B_WIDTH, B_KV_HEADS, True, None, None),
        (vb_ref, 3 * A_WIDTH + B_WIDTH + B_KV_WIDTH, B_KV_HEADS, False, None, None),
    )
    for out_ref, c0, heads, rotary, sc, a_idx in sections:
        for h0 in range(0, heads, 4):
            nh = min(4, heads - h0)
            lo = c0 + h0 * HEAD_DIM
            acc = jnp.dot(u, w_ref[:, lo:lo + nh * HEAD_DIM], preferred_element_type=F32)
            for j in range(nh):
                t = acc[:, j * HEAD_DIM:(j + 1) * HEAD_DIM]
                if rotary:
                    t = rope(t)
                if sc is not None:
                    t = t * sc
                cols = slice((h0 + j) * HEAD_DIM, (h0 + j + 1) * HEAD_DIM)
                out_ref[:, cols] = t.astype(BF16)
                if a_idx is not None:
                    stage_refs[a_idx][h0 + j] = t
        if a_idx is not None:
            for d, dref in zip(STRIDED_DILATIONS, dil_refs[a_idx]):
                for r in range(d):
                    for h in range(heads):
                        rows = stage_refs[a_idx][h, pl.ds(r, tm // d, stride=d), :]
                        lo = r * A_WIDTH + h * HEAD_DIM
                        dref[:, lo:lo + HEAD_DIM] = rows.astype(BF16)


def _qkv(x2d, g, w_in, cos, sa, sb, seq, tm):
    T, D = x2d.shape
    n_pos = seq // tm
    row = lambda i: (i, 0)
    tab = pl.BlockSpec((tm, LANES), lambda i: (i % n_pos, 0))
    widths = (A_WIDTH, A_WIDTH, A_WIDTH, B_WIDTH, B_KV_WIDTH, B_KV_WIDTH)
    out_specs = [pl.BlockSpec((tm, w), row) for w in widths]
    out_shape = [jax.ShapeDtypeStruct((T, w), BF16) for w in widths]
    for _ in range(3):
        for d in STRIDED_DILATIONS:
            out_specs.append(pl.BlockSpec((tm // d, d * A_WIDTH), row))
            out_shape.append(jax.ShapeDtypeStruct((T // d, d * A_WIDTH), BF16))
    return pl.pallas_call(
        _qkv_kernel,
        grid=(T // tm,),
        in_specs=[pl.BlockSpec((tm, D), row), _resident((1, D)), _resident(w_in.shape), tab, tab, tab],
        out_specs=out_specs,
        out_shape=out_shape,
        scratch_shapes=[pltpu.VMEM((A_HEADS, tm, HEAD_DIM), F32) for _ in range(3)],
        compiler_params=_params(("parallel",)),
        name="qkv",
    )(x2d, g, w_in, cos, sa, sb)


def _band_mask(i, lq, radius, length):
    shape = (lq, lq + 2 * radius)
    row = lax.broadcasted_iota(jnp.int32, shape, 0)
    col = lax.broadcasted_iota(jnp.int32, shape, 1)
    kpos = i * lq - radius + col
    return (jnp.abs(col - radius - row) <= radius) & (kpos >= 0) & (kpos < length)


def _window(prev_ref, cur_ref, next_ref, sl):
    return jnp.concatenate([prev_ref[:, sl], cur_ref[:, sl], next_ref[:, sl]], axis=0)


_NT = (((1,), (1,)), ((), ()))


def _attn_a_kernel(q_ref, kp_ref, kc_ref, kn_ref, vp_ref, vc_ref, vn_ref, o_ref, lse_ref,
                   *, length, lq, radius):
    valid = _band_mask(pl.program_id(2), lq, radius, length)
    lane = lax.broadcasted_iota(jnp.int32, (lq, LANES), 1)
    lse_all = jnp.zeros((lq, LANES), F32)
    for h in range(A_HEADS):
        sl = slice(h * HEAD_DIM, (h + 1) * HEAD_DIM)
        k = _window(kp_ref, kc_ref, kn_ref, sl)
        v = _window(vp_ref, vc_ref, vn_ref, sl)
        s = lax.dot_general(q_ref[:, sl], k, _NT, preferred_element_type=F32)
        s = jnp.where(valid, s, NEG_INF)
        m = jnp.max(s, axis=-1, keepdims=True)
        p = jnp.exp(s - m)
        l = jnp.sum(p, axis=-1, keepdims=True)
        o = jnp.dot(p.astype(BF16), v, preferred_element_type=F32)
        o_ref[:, sl] = (o / l).astype(o_ref.dtype)
        lse_all = jnp.where(lane == h, m + jnp.log(l), lse_all)
    lse_ref[...] = lse_all


def _attn_a(qa, ka, va, batch, seq, dil, radius, lq):
    ls = seq // dil
    lq = min(lq, ls)
    per = lq // radius
    n_halo = ls // radius
    view = lambda t: t.reshape(batch, ls, dil * A_WIDTH)
    cur = pl.BlockSpec((None, lq, A_WIDTH), lambda b, r, i: (b, i, r))
    prv = pl.BlockSpec((None, radius, A_WIDTH), lambda b, r, i: (b, jnp.maximum(i * per - 1, 0), r))
    nxt = pl.BlockSpec((None, radius, A_WIDTH),
                       lambda b, r, i: (b, jnp.minimum((i + 1) * per, n_halo - 1), r))
    o, lse = pl.pallas_call(
        functools.partial(_attn_a_kernel, length=ls, lq=lq, radius=radius),
        grid=(batch, dil, ls // lq),
        in_specs=[cur, prv, cur, nxt, prv, cur, nxt],
        out_specs=[cur, pl.BlockSpec((None, lq, LANES), lambda b, r, i: (b, i, r))],
        out_shape=[jax.ShapeDtypeStruct((batch, ls, dil * A_WIDTH), BF16),
                   jax.ShapeDtypeStruct((batch, ls, dil * LANES), F32)],
        compiler_params=_params(("parallel", "parallel", "parallel")),
        name=f"attn_a_d{dil}",
    )(view(qa), view(ka), view(ka), view(ka), view(va), view(va), view(va))
    return o.reshape(batch * ls, dil * A_WIDTH), lse.reshape(batch * ls, dil * LANES)


def _attn_b_kernel(sink_ref, q_ref, kp_ref, kc_ref, kn_ref, vp_ref, vc_ref, vn_ref, g_ref, o_ref,
                   *, length, lq, radius):
    valid = _band_mask(pl.program_id(1), lq, radius, length)
    outs = []
    for kv in range(B_KV_HEADS):
        sl = slice(kv * HEAD_DIM, (kv + 1) * HEAD_DIM)
        k = _window(kp_ref, kc_ref, kn_ref, sl)
        v = _window(vp_ref, vc_ref, vn_ref, sl)
        for j in range(B_GROUP):
            h = kv * B_GROUP + j
            s = lax.dot_general(q_ref[:, h * HEAD_DIM:(h + 1) * HEAD_DIM], k, _NT,
                                preferred_element_type=F32)
            s = jnp.where(valid, s, NEG_INF)
            sink = sink_ref[h]
            m = jnp.maximum(jnp.max(s, axis=-1, keepdims=True), sink)
            p = jnp.exp(s - m)
            l = jnp.sum(p, axis=-1, keepdims=True) + jnp.exp(sink - m)
            outs.append(jnp.dot(p.astype(BF16), v, preferred_element_type=F32) / l)
    o_ref[...] = _rms(jnp.concatenate(outs, axis=-1), g_ref[...]).astype(o_ref.dtype)


def _attn_b(qb, kb, vb, sink, g_out_b, batch, seq, lq):
    radius = B_RADIUS
    n_blk = seq // lq
    q3 = qb.reshape(batch, seq, B_WIDTH)
    k3 = kb.reshape(batch, seq, B_KV_WIDTH)
    v3 = vb.reshape(batch, seq, B_KV_WIDTH)
    qspec = pl.BlockSpec((None, lq, B_WIDTH), lambda b, i: (b, i, 0))
    cur = pl.BlockSpec((None, lq, B_KV_WIDTH), lambda b, i: (b, i, 0))
    prv = pl.BlockSpec((None, lq, B_KV_WIDTH), lambda b, i: (b, jnp.maximum(i - 1, 0), 0))
    nxt = pl.BlockSpec((None, lq, B_KV_WIDTH), lambda b, i: (b, jnp.minimum(i + 1, n_blk - 1), 0))
    out = pl.pallas_call(
        functools.partial(_attn_b_kernel, length=seq, lq=lq, radius=radius),
        grid=(batch, n_blk),
        in_specs=[pl.BlockSpec(memory_space=pltpu.SMEM), qspec, prv, cur, nxt, prv, cur, nxt,
                  _resident((1, B_WIDTH))],
        out_specs=qspec,
        out_shape=jax.ShapeDtypeStruct((batch, seq, B_WIDTH), BF16),
        compiler_params=_params(("parallel", "parallel")),
        name="attn_b",
    )(sink, q3, k3, k3, k3, v3, v3, v3, g_out_b)
    return out.reshape(batch * seq, B_WIDTH)


R_E1, R_E2, R_W1, R_W2, R_RANK1, R_RANK2 = range(6)


def _oproj_kernel(o1_ref, o4_ref, o16_ref, l1_ref, l4_ref, l16_ref, mb_ref, x_ref, wo_ref,
                  ga_ref, gf_ref, wr_ref, br_ref, cnt_in_ref, h_ref, u_ref, route_ref, cnt_ref,
                  o4n_ref, o16n_ref, l4n_ref, l16n_ref, *, sub):
    tm = x_ref.shape[0]

    @pl.when(pl.program_id(0) == 0)
    def _():
        cnt_ref[...] = cnt_in_ref[...]

    for d, o_src, o_dst, l_src, l_dst in ((STRIDED_DILATIONS[0], o4_ref, o4n_ref, l4_ref, l4n_ref),
                                          (STRIDED_DILATIONS[1], o16_ref, o16n_ref, l16_ref, l16n_ref)):
        for r in range(d):
            rows = pl.ds(r, tm // d, stride=d)
            for h in range(A_HEADS):
                lo = r * A_WIDTH + h * HEAD_DIM
                o_dst[h, rows, :] = o_src[:, lo:lo + HEAD_DIM].astype(F32)
            l_dst[rows, :] = l_src[:, r * LANES:(r + 1) * LANES]

    for c in range(tm // sub):
        _oproj_rows(slice(c * sub, (c + 1) * sub), o1_ref, o4n_ref, o16n_ref, l1_ref, l4n_ref, l16n_ref,
                    mb_ref, x_ref, wo_ref, ga_ref, gf_ref, wr_ref, br_ref, h_ref, u_ref, route_ref, cnt_ref)


def _oproj_rows(rs, o1_ref, o2_ref, o3_ref, l1_ref, l2_ref, l3_ref, mb_ref, x_ref, wo_ref,
                ga_ref, gf_ref, wr_ref, br_ref, h_ref, u_ref, route_ref, cnt_ref):
    tm = rs.stop - rs.start
    l1, l2, l3 = l1_ref[rs, :], l2_ref[rs, :], l3_ref[rs, :]
    mx = jnp.maximum(jnp.maximum(l1, l2), l3)
    e1, e2, e3 = jnp.exp(l1 - mx), jnp.exp(l2 - mx), jnp.exp(l3 - mx)
    inv = 1.0 / (e1 + e2 + e3)
    w1, w2, w3 = e1 * inv, e2 * inv, e3 * inv
    parts = []
    for h in range(A_HEADS):
        sl = slice(h * HEAD_DIM, (h + 1) * HEAD_DIM)
        bc = lambda w: jnp.broadcast_to(w[:, h:h + 1], (tm, HEAD_DIM))
        parts.append(bc(w1) * o1_ref[rs, sl].astype(F32) + bc(w2) * o2_ref[h, rs, :]
                     + bc(w3) * o3_ref[h, rs, :])
    mixed_a = _rms(jnp.concatenate(parts, axis=-1), ga_ref[...]).astype(BF16)
    h = (x_ref[rs, :]
         + jnp.dot(mixed_a, wo_ref[:A_WIDTH, :], preferred_element_type=F32)
         + jnp.dot(mb_ref[rs, :], wo_ref[A_WIDTH:, :], preferred_element_type=F32))
    h_ref[rs, :] = h
    u = _rms(h, gf_ref[...])
    u_ref[rs, :] = u

    u_hi = u.astype(BF16)
    u_lo = (u - u_hi.astype(F32)).astype(BF16)
    z = (jnp.dot(u_hi, wr_ref[...], preferred_element_type=F32)
         + jnp.dot(u_lo, wr_ref[...], preferred_element_type=F32))
    logits = z[:, :LANES] + z[:, LANES:] + br_ref[...]
    lane = lax.broadcasted_iota(jnp.int32, (tm, LANES), 1)
    big = jnp.int32(LANES)
    is_group = (lane >= N_EXPERTS) & (lane < N_EXPERTS + N_GROUPS)
    gmax = jnp.max(jnp.where(is_group, logits, -jnp.inf), axis=-1, keepdims=True)
    gidx = jnp.min(jnp.where(is_group & (logits == gmax), lane - N_EXPERTS, big), axis=-1, keepdims=True)
    ggate = 1.0 / jnp.sum(jnp.where(is_group, jnp.exp(logits - gmax), 0.0), axis=-1, keepdims=True)
    in_group = (lane >= gidx * EXPERTS_PER_GROUP) & (lane < (gidx + 1) * EXPERTS_PER_GROUP)
    v1 = jnp.max(jnp.where(in_group, logits, -jnp.inf), axis=-1, keepdims=True)
    i1 = jnp.min(jnp.where(in_group & (logits == v1), lane, big), axis=-1, keepdims=True)
    rest = in_group & (lane != i1)
    v2 = jnp.max(jnp.where(rest, logits, -jnp.inf), axis=-1, keepdims=True)
    i2 = jnp.min(jnp.where(rest & (logits == v2), lane, big), axis=-1, keepdims=True)
    t = jnp.exp(v2 - v1)
    tw1 = ggate / (1.0 + t)
    tw2 = ggate * t / (1.0 + t)

    onehot = ((lane == i1) | (lane == i2)).astype(F32)
    r_i = lax.broadcasted_iota(jnp.int32, (tm, tm), 0)
    c_i = lax.broadcasted_iota(jnp.int32, (tm, tm), 1)
    before = (c_i < r_i).astype(BF16)
    prefix = jnp.dot(before, onehot.astype(BF16), preferred_element_type=F32) + cnt_ref[...]
    rank1 = jnp.sum(jnp.where(lane == i1, prefix, 0.0), axis=-1, keepdims=True)
    rank2 = jnp.sum(jnp.where(lane == i2, prefix, 0.0), axis=-1, keepdims=True)
    cnt_ref[...] += jnp.sum(onehot, axis=0, keepdims=True)

    rec = jnp.zeros((tm, LANES), F32)
    for slot, val in ((R_E1, i1.astype(F32)), (R_E2, i2.astype(F32)), (R_W1, tw1), (R_W2, tw2),
                      (R_RANK1, rank1), (R_RANK2, rank2)):
        rec = jnp.where(lane == slot, val, rec)
    route_ref[rs, :] = rec


def _oproj(o_list, lse_list, mixed_b, x2d, w_o, g_out_a, g_ffn, w_route, b_route, cnt_in, tm, sub):
    T, D = x2d.shape
    row = lambda i: (i, 0)
    dils = (1,) + STRIDED_DILATIONS
    aspecs = [pl.BlockSpec((tm // d, d * A_WIDTH), row) for d in dils]
    lspecs = [pl.BlockSpec((tm // d, d * LANES), row) for d in dils]
    lspec = pl.BlockSpec((tm, LANES), row)
    dspec = pl.BlockSpec((tm, D), row)
    cspec = pl.BlockSpec((1, LANES), lambda i: (0, 0))
    return pl.pallas_call(
        functools.partial(_oproj_kernel, sub=sub),
        grid=(T // tm,),
        in_specs=[*aspecs, *lspecs, pl.BlockSpec((tm, B_WIDTH), row), dspec,
                  _resident(w_o.shape), _resident((1, A_WIDTH)), _resident((1, D)),
                  _resident(w_route.shape), _resident((1, LANES)), cspec],
        out_specs=[dspec, dspec, lspec, cspec],
        out_shape=[jax.ShapeDtypeStruct((T, D), F32), jax.ShapeDtypeStruct((T, D), F32),
                   jax.ShapeDtypeStruct((T, LANES), F32), jax.ShapeDtypeStruct((1, LANES), F32)],
        scratch_shapes=[pltpu.VMEM((A_HEADS, tm, HEAD_DIM), F32), pltpu.VMEM((A_HEADS, tm, HEAD_DIM), F32),
                        pltpu.VMEM((tm, LANES), F32), pltpu.VMEM((tm, LANES), F32)],
        compiler_params=_params(("arbitrary",)),
        name="oproj_route",
    )(*o_list, *lse_list, mixed_b, x2d, w_o, g_out_a, g_ffn, w_route, b_route, cnt_in)


def _row_copy(src_ref, src_row, dst_ref, dst_row, sem):
    return pltpu.make_async_copy(src_ref.at[pl.ds(src_row, 1)], dst_ref.at[pl.ds(dst_row, 1)], sem)


def _dispatch_kernel(pos_ref, fill_ref, *rest, tm, tiles_per_src, fill_rows):
    n_src = len(tiles_per_src)
    u_refs = rest[:n_src]
    xs_ref, sem, zero_ref, fill_sem = rest[n_src:]
    i = pl.program_id(0)

    @pl.when(i == 0)
    def _():
        zero_ref[...] = jnp.zeros_like(zero_ref)

        def tile_copy(k):
            start = pl.multiple_of(fill_ref[k], fill_rows)
            return pltpu.make_async_copy(zero_ref, xs_ref.at[pl.ds(start, fill_rows)], fill_sem)

        for k in range(fill_ref.shape[0]):
            @pl.when(fill_ref[k] >= 0)
            def _():
                tile_copy(k).start()

        for k in range(fill_ref.shape[0]):
            @pl.when(fill_ref[k] >= 0)
            def _():
                tile_copy(k).wait()

    def scatter(u_ref):
        def issue(j, carry):
            _row_copy(u_ref, j, xs_ref, pos_ref[0, j], sem).start()
            _row_copy(u_ref, j, xs_ref, pos_ref[0, tm + j], sem).start()
            return carry

        lax.fori_loop(0, tm, issue, 0, unroll=4)

        def drain(j, carry):
            _row_copy(u_ref, 0, xs_ref, 0, sem).wait()
            return carry

        lax.fori_loop(0, 2 * tm, drain, 0, unroll=8)

    first = 0
    for u_ref, n in zip(u_refs, tiles_per_src):
        @pl.when((i >= first) & (i < first + n))
        def _():
            scatter(u_ref)

        first += n


def _tile_positions(pos1, pos2, tm):
    n = pos1.shape[0] // tm
    return jnp.stack([pos1.reshape(n, tm), pos2.reshape(n, tm)], axis=1).reshape(n, 1, 2 * tm)


def _dispatch(us, pos_tiles, fill_starts, n_rows, tm, fill_rows):
    D = us[0].shape[1]
    tiles_per_src = tuple(u.shape[0] // tm for u in us)
    in_specs = [pl.BlockSpec((None, 1, 2 * tm), lambda i: (i, 0, 0), memory_space=pltpu.SMEM),
                pl.BlockSpec(memory_space=pltpu.SMEM)]
    first = 0
    for n in tiles_per_src:
        in_specs.append(pl.BlockSpec((tm, D), lambda i, first=first, n=n: (jnp.clip(i - first, 0, n - 1), 0)))
        first += n
    return pl.pallas_call(
        functools.partial(_dispatch_kernel, tm=tm, tiles_per_src=tiles_per_src, fill_rows=fill_rows),
        grid=(first,),
        in_specs=in_specs,
        out_specs=pl.BlockSpec(memory_space=pl.ANY),
        out_shape=jax.ShapeDtypeStruct((n_rows, D), F32),
        scratch_shapes=[pltpu.SemaphoreType.DMA(()), pltpu.VMEM((fill_rows, D), F32),
                        pltpu.SemaphoreType.DMA(())],
        compiler_params=_params(("arbitrary",)),
        name="moe_dispatch",
    )(pos_tiles, fill_starts, *us)


def _experts_kernel(te_ref, used_ref, x_ref, w1_ref, w3_ref, w2_ref, y_ref):
    del te_ref

    @pl.when(pl.program_id(0) < used_ref[0])
    def _():
        x = x_ref[...].astype(BF16)
        a = jnp.dot(x, w1_ref[...], preferred_element_type=F32)
        b = jnp.dot(x, w3_ref[...], preferred_element_type=F32)
        hid = (a * jax.nn.sigmoid(a)) * b
        y_ref[...] = jnp.dot(hid.astype(BF16), w2_ref[...], preferred_element_type=F32)

    @pl.when(pl.program_id(0) >= used_ref[0])
    def _():
        y_ref[...] = jnp.zeros_like(y_ref)


def _experts(xs, tile_expert, n_used, w1, w3, w2, tm):
    P, D = xs.shape
    F = w1.shape[2]
    row = lambda i, te, used: (jnp.minimum(i, used[0] - 1), 0)
    wmap = lambda i, te, used: (te[jnp.minimum(i, used[0] - 1)], 0, 0)
    grid_spec = pltpu.PrefetchScalarGridSpec(
        num_scalar_prefetch=2,
        grid=(P // tm,),
        in_specs=[pl.BlockSpec((tm, D), row),
                  pl.BlockSpec((None, D, F), wmap),
                  pl.BlockSpec((None, D, F), wmap),
                  pl.BlockSpec((None, F, D), wmap)],
        out_specs=pl.BlockSpec((tm, D), lambda i, te, used: (i, 0)),
    )
    return pl.pallas_call(
        _experts_kernel,
        grid_spec=grid_spec,
        out_shape=jax.ShapeDtypeStruct((P, D), F32),
        compiler_params=_params(("arbitrary",)),
        name="moe_experts",
    )(tile_expert, n_used, xs, w1, w3, w2)


def _ple_kernel(pos_ref, pos_next_ref, ys_ref, h_ref, route_ref, p_ref, gp_ref, wg_ref, bg_ref, wp_ref,
                gf_ref, o_ref, ybuf, sems, *, tm):
    i = pl.program_id(0)
    slot = i % 2

    def gather(p_ref_, s):
        def issue(j, carry):
            _row_copy(ys_ref, p_ref_[0, j], ybuf.at[s, 0], j, sems.at[s]).start()
            _row_copy(ys_ref, p_ref_[0, tm + j], ybuf.at[s, 1], j, sems.at[s]).start()
            return carry

        lax.fori_loop(0, tm, issue, 0, unroll=4)

    @pl.when(i == 0)
    def _():
        gather(pos_ref, 0)

    @pl.when(i + 1 < pl.num_programs(0))
    def _():
        gather(pos_next_ref, 1 - slot)

    def drain(j, carry):
        _row_copy(ys_ref, 0, ybuf.at[slot, 0], 0, sems.at[slot]).wait()
        return carry

    lax.fori_loop(0, 2 * tm, drain, 0, unroll=8)

    route = route_ref[...]
    h = (h_ref[...] + route[:, R_W1:R_W1 + 1] * ybuf[slot, 0] + route[:, R_W2:R_W2 + 1] * ybuf[slot, 1])
    u = _rms(h, gp_ref[...]).astype(BF16)
    gate = jax.nn.sigmoid(jnp.dot(u, wg_ref[...], preferred_element_type=F32) + bg_ref[...])
    proj = jnp.dot(p_ref[...].astype(BF16), wp_ref[...], preferred_element_type=F32)
    o_ref[...] = _rms(h + gate * proj, gf_ref[...])


def _ple(pos_tiles, ys, h, route, p2d, g_ple, w_gate, b_gate, w_proj, g_final, tm):
    T, D = h.shape
    P = p2d.shape[1]
    row = lambda i: (i, 0)
    n = T // tm
    return pl.pallas_call(
        functools.partial(_ple_kernel, tm=tm),
        grid=(n,),
        in_specs=[pl.BlockSpec((None, 1, 2 * tm), lambda i: (i, 0, 0), memory_space=pltpu.SMEM),
                  pl.BlockSpec((None, 1, 2 * tm), lambda i: (jnp.minimum(i + 1, n - 1), 0, 0),
                               memory_space=pltpu.SMEM),
                  pl.BlockSpec(memory_space=pl.ANY),
                  pl.BlockSpec((tm, D), row), pl.BlockSpec((tm, LANES), row), pl.BlockSpec((tm, P), row),
                  _resident((1, D)), _resident(w_gate.shape), _resident((1, D)), _resident(w_proj.shape),
                  _resident((1, D))],
        out_specs=pl.BlockSpec((tm, D), row),
        out_shape=jax.ShapeDtypeStruct((T, D), F32),
        scratch_shapes=[pltpu.VMEM((2, 2, tm, D), F32), pltpu.SemaphoreType.DMA((2,))],
        compiler_params=_params(("arbitrary",)),
        name="ple_final",
    )(pos_tiles, pos_tiles, ys, h, route, p2d, g_ple, w_gate, b_gate, w_proj, g_final)


def _rope_tables(seq):
    half = ROPE_DIM // 2
    inv = ROPE_THETA ** (-(jnp.arange(half, dtype=F32) * 2.0) / ROPE_DIM)
    ang = jnp.arange(seq, dtype=F32)[:, None] * inv[None, :]
    cos, sin = jnp.cos(ang), jnp.sin(ang)
    zeros = jnp.zeros((seq, LANES - ROPE_DIM), F32)
    z16 = jnp.zeros((seq, half), F32)
    cos_t = jnp.concatenate([cos, cos, jnp.ones((seq, LANES - ROPE_DIM), F32)], axis=-1)
    sa_t = jnp.concatenate([z16, sin, zeros], axis=-1)
    sb_t = jnp.concatenate([-sin, z16, zeros], axis=-1)
    return cos_t, sa_t, sb_t


MOE_TILE = 256
OPROJ_TILE = 512
PLE_TILE = 256
DISPATCH_TILE = 1024


def _mixer(x, wts, cnt_in):
    batch, seq, D = x.shape
    T = batch * seq
    x2d = x.reshape(T, D)
    cos, sa, sb = _rope_tables(seq)
    qa, ka, va, qb, kb, vb, *strided = _qkv(x2d, wts["g_attn"], wts["w_in"], cos, sa, sb, seq, tm=256)
    n_dil = len(STRIDED_DILATIONS)
    qkv_by_dil = {1: (qa, ka, va)}
    for n, d in enumerate(STRIDED_DILATIONS):
        qkv_by_dil[d] = tuple(strided[i * n_dil + n] for i in range(3))
    o_list, lse_list = [], []
    for window, dil in DILATED_CONFIGS:
        o, lse = _attn_a(*qkv_by_dil[dil], batch, seq, dil, window // (2 * dil), lq=128)
        o_list.append(o)
        lse_list.append(lse)
    mixed_b = _attn_b(qb, kb, vb, wts["sink"], wts["g_out_b"], batch, seq, lq=128)
    return _oproj(o_list, lse_list, mixed_b, x2d, wts["w_o"], wts["g_out_a"], wts["g_ffn"],
                  wts["w_route"], wts["b_route"], cnt_in, tm=OPROJ_TILE, sub=OPROJ_TILE // 2)


def _sorted_layout(counts, n_tiles, tm):
    cnt = counts[0, :N_EXPERTS].astype(jnp.int32)
    tiles = (cnt + tm - 1) // tm
    ends = jnp.cumsum(tiles)
    offsets = (ends - tiles) * tm
    tile_expert = jnp.sum(jnp.arange(n_tiles, dtype=jnp.int32)[:, None] >= ends[None, :], axis=1)
    tile_expert = jnp.minimum(tile_expert, N_EXPERTS - 1).astype(jnp.int32)
    n_used = ends[-1]
    last_tile_start = jnp.where(tiles > 0, (ends - 1) * tm, -1)
    tail = n_used + jnp.arange(N_EXPERTS, dtype=jnp.int32)
    tail_start = jnp.where(tail < n_tiles, tail * tm, -1)
    fill_starts = jnp.concatenate([last_tile_start, tail_start]).astype(jnp.int32)
    return offsets, tile_expert, n_used.reshape(1).astype(jnp.int32), fill_starts


def _positions(route, offsets):
    e1 = route[:, R_E1].astype(jnp.int32)
    e2 = route[:, R_E2].astype(jnp.int32)
    pos1 = jnp.take(offsets, e1) + route[:, R_RANK1].astype(jnp.int32)
    pos2 = jnp.take(offsets, e2) + route[:, R_RANK2].astype(jnp.int32)
    return pos1, pos2


def kernel(x_prompt, x_sample, p_prompt, p_sample, g_attn, w_in, g_out_a, g_out_b, sink, w_o, g_ffn, w_group, b_group, w_router, b_router, w1, w3, w2, g_ple, w_ple_gate, b_ple_gate, w_ple_proj, g_final):
    depth = w_in.shape[0]
    assert depth == 1
    l = 0
    D = w_in.shape[1]
    pad = LANES - N_EXPERTS - N_GROUPS
    w_route = jnp.pad(jnp.concatenate([w_router[l], w_group[l]], axis=-1), ((0, 0), (0, pad)))
    w_route_hi = w_route.astype(BF16)
    w_route_lo = (w_route - w_route_hi.astype(F32)).astype(BF16)
    wts = {
        "g_attn": g_attn[l].reshape(1, D),
        "w_in": w_in[l].astype(BF16),
        "g_out_a": g_out_a[l].reshape(1, A_WIDTH),
        "g_out_b": g_out_b[l].reshape(1, B_WIDTH),
        "sink": sink[l],
        "w_o": w_o[l].astype(BF16),
        "g_ffn": g_ffn[l].reshape(1, D),
        "w_route": jnp.concatenate([w_route_hi, w_route_lo], axis=-1),
        "b_route": jnp.pad(jnp.concatenate([b_router[l], b_group[l]]), (0, pad)).reshape(1, LANES),
        "w1": w1[l].astype(BF16),
        "w3": w3[l].astype(BF16),
        "w2": w2[l].astype(BF16),
        "g_ple": g_ple[l].reshape(1, D),
        "w_ple_gate": w_ple_gate[l].astype(BF16),
        "b_ple_gate": b_ple_gate[l].reshape(1, D),
        "w_ple_proj": w_ple_proj[l].astype(BF16),
        "g_final": g_final.reshape(1, D),
    }
    xs_list = (x_prompt, x_sample)
    ps_list = (p_prompt[l], p_sample[l])

    counts = jnp.zeros((1, LANES), F32)
    mixed = []
    for x in xs_list:
        h1, u2, route, counts = _mixer(x, wts, counts)
        mixed.append((h1, u2, route))

    n_pairs = 2 * sum(x.shape[0] * x.shape[1] for x in xs_list)
    n_tiles = n_pairs // MOE_TILE + N_EXPERTS
    offsets, tile_expert, n_used, fill_starts = _sorted_layout(counts, n_tiles, MOE_TILE)
    pos = [_positions(route, offsets) for _, _, route in mixed]
    pos_tiles = jnp.concatenate([_tile_positions(p1, p2, DISPATCH_TILE) for p1, p2 in pos], axis=0)
    xs = _dispatch([u2 for _, u2, _ in mixed], pos_tiles, fill_starts, n_tiles * MOE_TILE,
                   DISPATCH_TILE, MOE_TILE)
    ys = _experts(xs, tile_expert, n_used, wts["w1"], wts["w3"], wts["w2"], MOE_TILE)

    outs = []
    for x, p, (h1, u2, route), (pos1, pos2) in zip(xs_list, ps_list, mixed, pos):
        y = _ple(_tile_positions(pos1, pos2, PLE_TILE), ys, h1, route, p.reshape(h1.shape[0], -1),
                 wts["g_ple"], wts["w_ple_gate"], wts["b_ple_gate"], wts["w_ple_proj"], wts["g_final"],
                 PLE_TILE)
        outs.append(y.reshape(x.shape))
    return tuple(outs)
```

```python
import functools

import jax
import jax.numpy as jnp
import numpy as np
from jax import lax
from jax.experimental import pallas as pl
from jax.experimental.pallas import tpu as pltpu

HEAD_DIM = 128
A_HEADS = 8
A_WIDTH = A_HEADS * HEAD_DIM
B_HEADS = 8
B_KV_HEADS = 2
B_GROUP = B_HEADS // B_KV_HEADS
B_WIDTH = B_HEADS * HEAD_DIM
B_KV_WIDTH = B_KV_HEADS * HEAD_DIM
DILATED_CONFIGS = ((128, 1), (512, 4), (2048, 16))
B_RADIUS = 128
ROPE_THETA = 500000.0
ROPE_DIM = HEAD_DIM // 4
N_GROUPS = 4
EXPERTS_PER_GROUP = 8
N_EXPERTS = N_GROUPS * EXPERTS_PER_GROUP
RMS_EPS = 1e-6
NEG_INF = -1e30

LANES = 128
SUBLANES = 8
VMEM_LIMIT = 56 * 1024 * 1024

F32 = jnp.float32
BF16 = jnp.bfloat16


def _resident(shape):
    nd = len(shape)
    return pl.BlockSpec(shape, lambda *_: (0,) * nd, pipeline_mode=pl.Buffered(1))


def _params(sem):
    return pltpu.CompilerParams(dimension_semantics=sem, vmem_limit_bytes=VMEM_LIMIT)


def _rms(x, g):
    return x * lax.rsqrt(jnp.mean(x * x, axis=-1, keepdims=True) + RMS_EPS) * g


STRIDED_DILATIONS = tuple(d for _, d in DILATED_CONFIGS if d > 1)


def _qkv_kernel(x_ref, g_ref, w_ref, cos_ref, sa_ref, sb_ref,
                qa_ref, ka_ref, va_ref, qb_ref, kb_ref, vb_ref, *rest):
    n_dil = len(STRIDED_DILATIONS)
    dil_refs = [rest[i * n_dil:(i + 1) * n_dil] for i in range(3)]
    stage_refs = rest[3 * n_dil:]
    tm = x_ref.shape[0]
    u = _rms(x_ref[...], g_ref[...]).astype(BF16)
    cos = cos_ref[...]
    sa = sa_ref[...]
    sb = sb_ref[...]
    half = ROPE_DIM // 2
    scale = HEAD_DIM ** -0.5

    def rope(t):
        return t * cos + pltpu.roll(t, half, 1) * sa + pltpu.roll(t, LANES - half, 1) * sb

    sections = (
        (qa_ref, 0, A_HEADS, True, scale, 0),
        (ka_ref, A_WIDTH, A_HEADS, True, None, 1),
        (va_ref, 2 * A_WIDTH, A_HEADS, False, None, 2),
        (qb_ref, 3 * A_WIDTH, B_HEADS, True, scale, None),
        (kb_ref, 3 * A_WIDTH + B_WIDTH, B_KV_HEADS, True, None, None),
        (vb_ref, 3 * A_WIDTH + B_WIDTH + B_KV_WIDTH, B_KV_HEADS, False, None, None),
    )
    for out_ref, c0, heads, rotary, sc, a_idx in sections:
        for h0 in range(0, heads, 4):
            nh = min(4, heads - h0)
            lo = c0 + h0 * HEAD_DIM
            acc = jnp.dot(u, w_ref[:, lo:lo + nh * HEAD_DIM], preferred_element_type=F32)
            for j in range(nh):
                t = acc[:, j * HEAD_DIM:(j + 1) * HEAD_DIM]
                if rotary:
                    t = rope(t)
                if sc is not None:
                    t = t * sc
                cols = slice((h0 + j) * HEAD_DIM, (h0 + j + 1) * HEAD_DIM)
                out_ref[:, cols] = t.astype(BF16)
                if a_idx is not None:
                    stage_refs[a_idx][h0 + j] = t
        if a_idx is not None:
            for d, dref in zip(STRIDED_DILATIONS, dil_refs[a_idx]):
                for r in range(d):
                    for h in range(heads):
                        rows = stage_refs[a_idx][h, pl.ds(r, tm // d, stride=d), :]
                        lo = r * A_WIDTH + h * HEAD_DIM
                        dref[:, lo:lo + HEAD_DIM] = rows.astype(BF16)


def _qkv(x2d, g, w_in, cos, sa, sb, seq, tm):
    T, D = x2d.shape
    n_pos = seq // tm
    row = lambda i: (i, 0)
    tab = pl.BlockSpec((tm, LANES), lambda i: (i % n_pos, 0))
    widths = (A_WIDTH, A_WIDTH, A_WIDTH, B_WIDTH, B_KV_WIDTH, B_KV_WIDTH)
    out_specs = [pl.BlockSpec((tm, w), row) for w in widths]
    out_shape = [jax.ShapeDtypeStruct((T, w), BF16) for w in widths]
    for _ in range(3):
        for d in STRIDED_DILATIONS:
            out_specs.append(pl.BlockSpec((tm // d, d * A_WIDTH), row))
            out_shape.append(jax.ShapeDtypeStruct((T // d, d * A_WIDTH), BF16))
    return pl.pallas_call(
        _qkv_kernel,
        grid=(T // tm,),
        in_specs=[pl.BlockSpec((tm, D), row), _resident((1, D)), _resident(w_in.shape), tab, tab, tab],
        out_specs=out_specs,
        out_shape=out_shape,
        scratch_shapes=[pltpu.VMEM((A_HEADS, tm, HEAD_DIM), F32) for _ in range(3)],
        compiler_params=_params(("parallel",)),
        name="qkv",
    )(x2d, g, w_in, cos, sa, sb)


ATTN_SUB = 128
ATTN_BLOCK = 1024


def _band_mask(first_q, sub, radius, length):
    shape = (sub, sub + 2 * radius)
    row = lax.broadcasted_iota(jnp.int32, shape, 0)
    col = lax.broadcasted_iota(jnp.int32, shape, 1)
    kpos = first_q - radius + col
    return (jnp.abs(col - radius - row) <= radius) & (kpos >= 0) & (kpos < length)


def _fill_window(ext_ref, prev_ref, cur_ref, next_ref, radius):
    n = cur_ref.shape[0]
    ext_ref[0:radius, :] = prev_ref[...]
    ext_ref[radius:radius + n, :] = cur_ref[...]
    ext_ref[radius + n:, :] = next_ref[...]


_NT = (((1,), (1,)), ((), ()))


def _attn_a_kernel(q_ref, kp_ref, kc_ref, kn_ref, vp_ref, vc_ref, vn_ref, o_ref, lse_ref, kx_ref, vx_ref,
                   *, length, lq, radius, n_res):
    sub = min(ATTN_SUB, lq)
    _fill_window(kx_ref, kp_ref, kc_ref, kn_ref, radius)
    _fill_window(vx_ref, vp_ref, vc_ref, vn_ref, radius)
    lane = lax.broadcasted_iota(jnp.int32, (sub, LANES), 1)
    for j in range(lq // sub):
        valid = _band_mask(pl.program_id(2) * lq + j * sub, sub, radius, length)
        q_rows = slice(j * sub, (j + 1) * sub)
        k_rows = slice(j * sub, (j + 1) * sub + 2 * radius)
        for res in range(n_res):
            lse_all = jnp.zeros((sub, LANES), F32)
            for h in range(A_HEADS):
                sl = slice(res * A_WIDTH + h * HEAD_DIM, res * A_WIDTH + (h + 1) * HEAD_DIM)
                s = lax.dot_general(q_ref[q_rows, sl], kx_ref[k_rows, sl], _NT, preferred_element_type=F32)
                s = jnp.where(valid, s, NEG_INF)
                m = jnp.max(s, axis=-1, keepdims=True)
                p = jnp.exp(s - m)
                l = jnp.sum(p, axis=-1, keepdims=True)
                o = jnp.dot(p.astype(BF16), vx_ref[k_rows, sl], preferred_element_type=F32)
                o_ref[q_rows, sl] = (o / l).astype(o_ref.dtype)
                lse_all = jnp.where(lane == h, m + jnp.log(l), lse_all)
            lse_ref[q_rows, res * LANES:(res + 1) * LANES] = lse_all


def _attn_a(qa, ka, va, batch, seq, dil, radius):
    ls = seq // dil
    lq = min(ATTN_BLOCK, ls)
    n_res = min(dil, ATTN_BLOCK // lq)
    per = lq // radius
    n_halo = ls // radius
    view = lambda t: t.reshape(batch, ls, dil * A_WIDTH)
    w = n_res * A_WIDTH
    cur = pl.BlockSpec((None, lq, w), lambda b, r, i: (b, i, r))
    prv = pl.BlockSpec((None, radius, w), lambda b, r, i: (b, jnp.maximum(i * per - 1, 0), r))
    nxt = pl.BlockSpec((None, radius, w), lambda b, r, i: (b, jnp.minimum((i + 1) * per, n_halo - 1), r))
    o, lse = pl.pallas_call(
        functools.partial(_attn_a_kernel, length=ls, lq=lq, radius=radius, n_res=n_res),
        grid=(batch, dil // n_res, ls // lq),
        in_specs=[cur, prv, cur, nxt, prv, cur, nxt],
        out_specs=[cur, pl.BlockSpec((None, lq, n_res * LANES), lambda b, r, i: (b, i, r))],
        out_shape=[jax.ShapeDtypeStruct((batch, ls, dil * A_WIDTH), BF16),
                   jax.ShapeDtypeStruct((batch, ls, dil * LANES), F32)],
        scratch_shapes=[pltpu.VMEM((lq + 2 * radius, w), BF16), pltpu.VMEM((lq + 2 * radius, w), BF16)],
        compiler_params=_params(("parallel", "parallel", "parallel")),
        name=f"attn_a_d{dil}",
    )(view(qa), view(ka), view(ka), view(ka), view(va), view(va), view(va))
    return o.reshape(batch * ls, dil * A_WIDTH), lse.reshape(batch * ls, dil * LANES)


def _attn_b_kernel(sink_ref, q_ref, kp_ref, kc_ref, kn_ref, vp_ref, vc_ref, vn_ref, g_ref, o_ref,
                   kx_ref, vx_ref, *, length, lq, radius):
    sub = ATTN_SUB
    _fill_window(kx_ref, kp_ref, kc_ref, kn_ref, radius)
    _fill_window(vx_ref, vp_ref, vc_ref, vn_ref, radius)
    for j in range(lq // sub):
        valid = _band_mask(pl.program_id(1) * lq + j * sub, sub, radius, length)
        q_rows = slice(j * sub, (j + 1) * sub)
        k_rows = slice(j * sub, (j + 1) * sub + 2 * radius)
        outs = []
        for h in range(B_HEADS):
            kv = slice((h // B_GROUP) * HEAD_DIM, (h // B_GROUP + 1) * HEAD_DIM)
            s = lax.dot_general(q_ref[q_rows, h * HEAD_DIM:(h + 1) * HEAD_DIM], kx_ref[k_rows, kv], _NT,
                                preferred_element_type=F32)
            s = jnp.where(valid, s, NEG_INF)
            sink = sink_ref[h]
            m = jnp.maximum(jnp.max(s, axis=-1, keepdims=True), sink)
            p = jnp.exp(s - m)
            l = jnp.sum(p, axis=-1, keepdims=True) + jnp.exp(sink - m)
            outs.append(jnp.dot(p.astype(BF16), vx_ref[k_rows, kv], preferred_element_type=F32) / l)
        o_ref[q_rows, :] = _rms(jnp.concatenate(outs, axis=-1), g_ref[...]).astype(o_ref.dtype)


def _attn_b(qb, kb, vb, sink, g_out_b, batch, seq):
    radius = B_RADIUS
    lq = min(ATTN_BLOCK, seq)
    per = lq // radius
    n_halo = seq // radius
    q3 = qb.reshape(batch, seq, B_WIDTH)
    k3 = kb.reshape(batch, seq, B_KV_WIDTH)
    v3 = vb.reshape(batch, seq, B_KV_WIDTH)
    qspec = pl.BlockSpec((None, lq, B_WIDTH), lambda b, i: (b, i, 0))
    cur = pl.BlockSpec((None, lq, B_KV_WIDTH), lambda b, i: (b, i, 0))
    prv = pl.BlockSpec((None, radius, B_KV_WIDTH), lambda b, i: (b, jnp.maximum(i * per - 1, 0), 0))
    nxt = pl.BlockSpec((None, radius, B_KV_WIDTH),
                       lambda b, i: (b, jnp.minimum((i + 1) * per, n_halo - 1), 0))
    ext = pltpu.VMEM((lq + 2 * radius, B_KV_WIDTH), BF16)
    out = pl.pallas_call(
        functools.partial(_attn_b_kernel, length=seq, lq=lq, radius=radius),
        grid=(batch, seq // lq),
        in_specs=[pl.BlockSpec(memory_space=pltpu.SMEM), qspec, prv, cur, nxt, prv, cur, nxt,
                  _resident((1, B_WIDTH))],
        out_specs=qspec,
        out_shape=jax.ShapeDtypeStruct((batch, seq, B_WIDTH), BF16),
        scratch_shapes=[ext, ext],
        compiler_params=_params(("parallel", "parallel")),
        name="attn_b",
    )(sink, q3, k3, k3, k3, v3, v3, v3, g_out_b)
    return out.reshape(batch * seq, B_WIDTH)


R_E1, R_E2, R_W1, R_W2, R_RANK1, R_RANK2 = range(6)


def _oproj_kernel(o1_ref, o4_ref, o16_ref, l1_ref, l4_ref, l16_ref, mb_ref, x_ref, wo_ref,
                  ga_ref, gf_ref, wr_ref, br_ref, cnt_in_ref, h_ref, u_ref, route_ref, cnt_ref,
                  o4n_ref, o16n_ref, l4n_ref, l16n_ref, *, sub):
    tm = x_ref.shape[0]

    @pl.when(pl.program_id(0) == 0)
    def _():
        cnt_ref[...] = cnt_in_ref[...]

    for d, o_src, o_dst, l_src, l_dst in ((STRIDED_DILATIONS[0], o4_ref, o4n_ref, l4_ref, l4n_ref),
                                          (STRIDED_DILATIONS[1], o16_ref, o16n_ref, l16_ref, l16n_ref)):
        for r in range(d):
            rows = pl.ds(r, tm // d, stride=d)
            for h in range(A_HEADS):
                lo = r * A_WIDTH + h * HEAD_DIM
                o_dst[h, rows, :] = o_src[:, lo:lo + HEAD_DIM].astype(F32)
            l_dst[rows, :] = l_src[:, r * LANES:(r + 1) * LANES]

    for c in range(tm // sub):
        _oproj_rows(slice(c * sub, (c + 1) * sub), o1_ref, o4n_ref, o16n_ref, l1_ref, l4n_ref, l16n_ref,
                    mb_ref, x_ref, wo_ref, ga_ref, gf_ref, wr_ref, br_ref, h_ref, u_ref, route_ref, cnt_ref)


def _oproj_rows(rs, o1_ref, o2_ref, o3_ref, l1_ref, l2_ref, l3_ref, mb_ref, x_ref, wo_ref,
                ga_ref, gf_ref, wr_ref, br_ref, h_ref, u_ref, route_ref, cnt_ref):
    tm = rs.stop - rs.start
    l1, l2, l3 = l1_ref[rs, :], l2_ref[rs, :], l3_ref[rs, :]
    mx = jnp.maximum(jnp.maximum(l1, l2), l3)
    e1, e2, e3 = jnp.exp(l1 - mx), jnp.exp(l2 - mx), jnp.exp(l3 - mx)
    inv = 1.0 / (e1 + e2 + e3)
    w1, w2, w3 = e1 * inv, e2 * inv, e3 * inv
    parts = []
    for h in range(A_HEADS):
        sl = slice(h * HEAD_DIM, (h + 1) * HEAD_DIM)
        bc = lambda w: jnp.broadcast_to(w[:, h:h + 1], (tm, HEAD_DIM))
        parts.append(bc(w1) * o1_ref[rs, sl].astype(F32) + bc(w2) * o2_ref[h, rs, :]
                     + bc(w3) * o3_ref[h, rs, :])
    mixed_a = _rms(jnp.concatenate(parts, axis=-1), ga_ref[...]).astype(BF16)
    h = (x_ref[rs, :]
         + jnp.dot(mixed_a, wo_ref[:A_WIDTH, :], preferred_element_type=F32)
         + jnp.dot(mb_ref[rs, :], wo_ref[A_WIDTH:, :], preferred_element_type=F32))
    h_ref[rs, :] = h
    u = _rms(h, gf_ref[...])
    u_ref[rs, :] = u

    u_hi = u.astype(BF16)
    u_lo = (u - u_hi.astype(F32)).astype(BF16)
    z = (jnp.dot(u_hi, wr_ref[...], preferred_element_type=F32)
         + jnp.dot(u_lo, wr_ref[...], preferred_element_type=F32))
    logits = z[:, :LANES] + z[:, LANES:] + br_ref[...]
    lane = lax.broadcasted_iota(jnp.int32, (tm, LANES), 1)
    big = jnp.int32(LANES)
    is_group = (lane >= N_EXPERTS) & (lane < N_EXPERTS + N_GROUPS)
    gmax = jnp.max(jnp.where(is_group, logits, -jnp.inf), axis=-1, keepdims=True)
    gidx = jnp.min(jnp.where(is_group & (logits == gmax), lane - N_EXPERTS, big), axis=-1, keepdims=True)
    ggate = 1.0 / jnp.sum(jnp.where(is_group, jnp.exp(logits - gmax), 0.0), axis=-1, keepdims=True)
    in_group = (lane >= gidx * EXPERTS_PER_GROUP) & (lane < (gidx + 1) * EXPERTS_PER_GROUP)
    v1 = jnp.max(jnp.where(in_group, logits, -jnp.inf), axis=-1, keepdims=True)
    i1 = jnp.min(jnp.where(in_group & (logits == v1), lane, big), axis=-1, keepdims=True)
    rest = in_group & (lane != i1)
    v2 = jnp.max(jnp.where(rest, logits, -jnp.inf), axis=-1, keepdims=True)
    i2 = jnp.min(jnp.where(rest & (logits == v2), lane, big), axis=-1, keepdims=True)
    t = jnp.exp(v2 - v1)
    tw1 = ggate / (1.0 + t)
    tw2 = ggate * t / (1.0 + t)

    onehot = ((lane == i1) | (lane == i2)).astype(F32)
    r_i = lax.broadcasted_iota(jnp.int32, (tm, tm), 0)
    c_i = lax.broadcasted_iota(jnp.int32, (tm, tm), 1)
    before = (c_i < r_i).astype(BF16)
    prefix = jnp.dot(before, onehot.astype(BF16), preferred_element_type=F32) + cnt_ref[...]
    rank1 = jnp.sum(jnp.where(lane == i1, prefix, 0.0), axis=-1, keepdims=True)
    rank2 = jnp.sum(jnp.where(lane == i2, prefix, 0.0), axis=-1, keepdims=True)
    cnt_ref[...] += jnp.sum(onehot, axis=0, keepdims=True)

    rec = jnp.zeros((tm, LANES), F32)
    for slot, val in ((R_E1, i1.astype(F32)), (R_E2, i2.astype(F32)), (R_W1, tw1), (R_W2, tw2),
                      (R_RANK1, rank1), (R_RANK2, rank2)):
        rec = jnp.where(lane == slot, val, rec)
    route_ref[rs, :] = rec


def _oproj(o_list, lse_list, mixed_b, x2d, w_o, g_out_a, g_ffn, w_route, b_route, cnt_in, tm, sub):
    T, D = x2d.shape
    row = lambda i: (i, 0)
    dils = (1,) + STRIDED_DILATIONS
    aspecs = [pl.BlockSpec((tm // d, d * A_WIDTH), row) for d in dils]
    lspecs = [pl.BlockSpec((tm // d, d * LANES), row) for d in dils]
    lspec = pl.BlockSpec((tm, LANES), row)
    dspec = pl.BlockSpec((tm, D), row)
    cspec = pl.BlockSpec((1, LANES), lambda i: (0, 0))
    return pl.pallas_call(
        functools.partial(_oproj_kernel, sub=sub),
        grid=(T // tm,),
        in_specs=[*aspecs, *lspecs, pl.BlockSpec((tm, B_WIDTH), row), dspec,
                  _resident(w_o.shape), _resident((1, A_WIDTH)), _resident((1, D)),
                  _resident(w_route.shape), _resident((1, LANES)), cspec],
        out_specs=[dspec, dspec, lspec, cspec],
        out_shape=[jax.ShapeDtypeStruct((T, D), F32), jax.ShapeDtypeStruct((T, D), F32),
                   jax.ShapeDtypeStruct((T, LANES), F32), jax.ShapeDtypeStruct((1, LANES), F32)],
        scratch_shapes=[pltpu.VMEM((A_HEADS, tm, HEAD_DIM), F32), pltpu.VMEM((A_HEADS, tm, HEAD_DIM), F32),
                        pltpu.VMEM((tm, LANES), F32), pltpu.VMEM((tm, LANES), F32)],
        compiler_params=_params(("arbitrary",)),
        name="oproj_route",
    )(*o_list, *lse_list, mixed_b, x2d, w_o, g_out_a, g_ffn, w_route, b_route, cnt_in)


def _row_copy(src_ref, src_row, dst_ref, dst_row, sem):
    return pltpu.make_async_copy(src_ref.at[pl.ds(src_row, 1)], dst_ref.at[pl.ds(dst_row, 1)], sem)


def _dispatch_kernel(pos_ref, fill_ref, *rest, tm, tiles_per_src, fill_rows):
    n_src = len(tiles_per_src)
    u_refs = rest[:n_src]
    xs_ref, sem, zero_ref, fill_sem = rest[n_src:]
    i = pl.program_id(0)

    @pl.when(i == 0)
    def _():
        zero_ref[...] = jnp.zeros_like(zero_ref)

        def tile_copy(k):
            start = pl.multiple_of(fill_ref[k], fill_rows)
            return pltpu.make_async_copy(zero_ref, xs_ref.at[pl.ds(start, fill_rows)], fill_sem)

        for k in range(fill_ref.shape[0]):
            @pl.when(fill_ref[k] >= 0)
            def _():
                tile_copy(k).start()

        for k in range(fill_ref.shape[0]):
            @pl.when(fill_ref[k] >= 0)
            def _():
                tile_copy(k).wait()

    def scatter(u_ref):
        def issue(g, carry):
            base = pl.multiple_of(g * SUBLANES, SUBLANES)
            for k in range(SUBLANES):
                j = base + k
                _row_copy(u_ref, j, xs_ref, pos_ref[0, j], sem).start()
                _row_copy(u_ref, j, xs_ref, pos_ref[0, tm + j], sem).start()
            return carry

        lax.fori_loop(0, tm // SUBLANES, issue, 0)

        def drain(j, carry):
            _row_copy(u_ref, 0, xs_ref, 0, sem).wait()
            return carry

        lax.fori_loop(0, 2 * tm, drain, 0, unroll=8)

    first = 0
    for u_ref, n in zip(u_refs, tiles_per_src):
        @pl.when((i >= first) & (i < first + n))
        def _():
            scatter(u_ref)

        first += n


def _tile_positions(pos1, pos2, tm):
    n = pos1.shape[0] // tm
    return jnp.stack([pos1.reshape(n, tm), pos2.reshape(n, tm)], axis=1).reshape(n, 1, 2 * tm)


def _dispatch(us, pos_tiles, fill_starts, n_rows, tm, fill_rows):
    D = us[0].shape[1]
    tiles_per_src = tuple(u.shape[0] // tm for u in us)
    in_specs = [pl.BlockSpec((None, 1, 2 * tm), lambda i: (i, 0, 0), memory_space=pltpu.SMEM),
                pl.BlockSpec(memory_space=pltpu.SMEM)]
    first = 0
    for n in tiles_per_src:
        in_specs.append(pl.BlockSpec((tm, D), lambda i, first=first, n=n: (jnp.clip(i - first, 0, n - 1), 0)))
        first += n
    return pl.pallas_call(
        functools.partial(_dispatch_kernel, tm=tm, tiles_per_src=tiles_per_src, fill_rows=fill_rows),
        grid=(first,),
        in_specs=in_specs,
        out_specs=pl.BlockSpec(memory_space=pl.ANY),
        out_shape=jax.ShapeDtypeStruct((n_rows, D), F32),
        scratch_shapes=[pltpu.SemaphoreType.DMA(()), pltpu.VMEM((fill_rows, D), F32),
                        pltpu.SemaphoreType.DMA(())],
        compiler_params=_params(("arbitrary",)),
        name="moe_dispatch",
    )(pos_tiles, fill_starts, *us)


def _experts_kernel(te_ref, used_ref, x_ref, w1_ref, w3_ref, w2_ref, y_ref):
    del te_ref

    @pl.when(pl.program_id(0) < used_ref[0])
    def _():
        x = x_ref[...].astype(BF16)
        a = jnp.dot(x, w1_ref[...], preferred_element_type=F32)
        b = jnp.dot(x, w3_ref[...], preferred_element_type=F32)
        hid = (a * jax.nn.sigmoid(a)) * b
        y_ref[...] = jnp.dot(hid.astype(BF16), w2_ref[...], preferred_element_type=F32)

    @pl.when(pl.program_id(0) >= used_ref[0])
    def _():
        y_ref[...] = jnp.zeros_like(y_ref)


def _experts(xs, tile_expert, n_used, w1, w3, w2, tm):
    P, D = xs.shape
    F = w1.shape[2]
    row = lambda i, te, used: (jnp.minimum(i, used[0] - 1), 0)
    wmap = lambda i, te, used: (te[jnp.minimum(i, used[0] - 1)], 0, 0)
    grid_spec = pltpu.PrefetchScalarGridSpec(
        num_scalar_prefetch=2,
        grid=(P // tm,),
        in_specs=[pl.BlockSpec((tm, D), row),
                  pl.BlockSpec((None, D, F), wmap),
                  pl.BlockSpec((None, D, F), wmap),
                  pl.BlockSpec((None, F, D), wmap)],
        out_specs=pl.BlockSpec((tm, D), lambda i, te, used: (i, 0)),
    )
    return pl.pallas_call(
        _experts_kernel,
        grid_spec=grid_spec,
        out_shape=jax.ShapeDtypeStruct((P, D), F32),
        compiler_params=_params(("arbitrary",)),
        name="moe_experts",
    )(tile_expert, n_used, xs, w1, w3, w2)


def _ple_kernel(pos_ref, pos_next_ref, ys_ref, h_ref, route_ref, p_ref, gp_ref, wg_ref, bg_ref, wp_ref,
                gf_ref, o_ref, ybuf, sems, *, tm):
    i = pl.program_id(0)
    slot = i % 2

    def gather(p_ref_, s):
        def issue(g, carry):
            base = pl.multiple_of(g * SUBLANES, SUBLANES)
            for k in range(SUBLANES):
                j = base + k
                _row_copy(ys_ref, p_ref_[0, j], ybuf.at[s, 0], j, sems.at[s]).start()
                _row_copy(ys_ref, p_ref_[0, tm + j], ybuf.at[s, 1], j, sems.at[s]).start()
            return carry

        lax.fori_loop(0, tm // SUBLANES, issue, 0)

    @pl.when(i == 0)
    def _():
        gather(pos_ref, 0)

    @pl.when(i + 1 < pl.num_programs(0))
    def _():
        gather(pos_next_ref, 1 - slot)

    def drain(j, carry):
        _row_copy(ys_ref, 0, ybuf.at[slot, 0], 0, sems.at[slot]).wait()
        return carry

    lax.fori_loop(0, 2 * tm, drain, 0, unroll=8)

    route = route_ref[...]
    h = (h_ref[...] + route[:, R_W1:R_W1 + 1] * ybuf[slot, 0] + route[:, R_W2:R_W2 + 1] * ybuf[slot, 1])
    u = _rms(h, gp_ref[...]).astype(BF16)
    gate = jax.nn.sigmoid(jnp.dot(u, wg_ref[...], preferred_element_type=F32) + bg_ref[...])
    proj = jnp.dot(p_ref[...].astype(BF16), wp_ref[...], preferred_element_type=F32)
    o_ref[...] = _rms(h + gate * proj, gf_ref[...])


def _ple(pos_tiles, ys, h, route, p2d, g_ple, w_gate, b_gate, w_proj, g_final, tm):
    T, D = h.shape
    P = p2d.shape[1]
    row = lambda i: (i, 0)
    n = T // tm
    return pl.pallas_call(
        functools.partial(_ple_kernel, tm=tm),
        grid=(n,),
        in_specs=[pl.BlockSpec((None, 1, 2 * tm), lambda i: (i, 0, 0), memory_space=pltpu.SMEM),
                  pl.BlockSpec((None, 1, 2 * tm), lambda i: (jnp.minimum(i + 1, n - 1), 0, 0),
                               memory_space=pltpu.SMEM),
                  pl.BlockSpec(memory_space=pl.ANY),
                  pl.BlockSpec((tm, D), row), pl.BlockSpec((tm, LANES), row), pl.BlockSpec((tm, P), row),
                  _resident((1, D)), _resident(w_gate.shape), _resident((1, D)), _resident(w_proj.shape),
                  _resident((1, D))],
        out_specs=pl.BlockSpec((tm, D), row),
        out_shape=jax.ShapeDtypeStruct((T, D), F32),
        scratch_shapes=[pltpu.VMEM((2, 2, tm, D), F32), pltpu.SemaphoreType.DMA((2,))],
        compiler_params=_params(("arbitrary",)),
        name="ple_final",
    )(pos_tiles, pos_tiles, ys, h, route, p2d, g_ple, w_gate, b_gate, w_proj, g_final)


def _rope_tables(seq):
    half = ROPE_DIM // 2
    inv = ROPE_THETA ** (-(jnp.arange(half, dtype=F32) * 2.0) / ROPE_DIM)
    ang = jnp.arange(seq, dtype=F32)[:, None] * inv[None, :]
    cos, sin = jnp.cos(ang), jnp.sin(ang)
    zeros = jnp.zeros((seq, LANES - ROPE_DIM), F32)
    z16 = jnp.zeros((seq, half), F32)
    cos_t = jnp.concatenate([cos, cos, jnp.ones((seq, LANES - ROPE_DIM), F32)], axis=-1)
    sa_t = jnp.concatenate([z16, sin, zeros], axis=-1)
    sb_t = jnp.concatenate([-sin, z16, zeros], axis=-1)
    return cos_t, sa_t, sb_t


MOE_TILE = 256
OPROJ_TILE = 512
PLE_TILE = 256
DISPATCH_TILE = 1024


def _mixer(x, wts, cnt_in):
    batch, seq, D = x.shape
    T = batch * seq
    x2d = x.reshape(T, D)
    cos, sa, sb = _rope_tables(seq)
    qa, ka, va, qb, kb, vb, *strided = _qkv(x2d, wts["g_attn"], wts["w_in"], cos, sa, sb, seq, tm=256)
    n_dil = len(STRIDED_DILATIONS)
    qkv_by_dil = {1: (qa, ka, va)}
    for n, d in enumerate(STRIDED_DILATIONS):
        qkv_by_dil[d] = tuple(strided[i * n_dil + n] for i in range(3))
    o_list, lse_list = [], []
    for window, dil in DILATED_CONFIGS:
        o, lse = _attn_a(*qkv_by_dil[dil], batch, seq, dil, window // (2 * dil))
        o_list.append(o)
        lse_list.append(lse)
    mixed_b = _attn_b(qb, kb, vb, wts["sink"], wts["g_out_b"], batch, seq)
    return _oproj(o_list, lse_list, mixed_b, x2d, wts["w_o"], wts["g_out_a"], wts["g_ffn"],
                  wts["w_route"], wts["b_route"], cnt_in, tm=OPROJ_TILE, sub=OPROJ_TILE // 2)


def _sorted_layout(counts, n_tiles, tm):
    cnt = counts[0, :N_EXPERTS].astype(jnp.int32)
    tiles = (cnt + tm - 1) // tm
    ends = jnp.cumsum(tiles)
    offsets = (ends - tiles) * tm
    tile_expert = jnp.sum(jnp.arange(n_tiles, dtype=jnp.int32)[:, None] >= ends[None, :], axis=1)
    tile_expert = jnp.minimum(tile_expert, N_EXPERTS - 1).astype(jnp.int32)
    n_used = ends[-1]
    last_tile_start = jnp.where(tiles > 0, (ends - 1) * tm, -1)
    tail = n_used + jnp.arange(N_EXPERTS, dtype=jnp.int32)
    tail_start = jnp.where(tail < n_tiles, tail * tm, -1)
    fill_starts = jnp.concatenate([last_tile_start, tail_start]).astype(jnp.int32)
    return offsets, tile_expert, n_used.reshape(1).astype(jnp.int32), fill_starts


def _positions(route, offsets):
    e1 = route[:, R_E1].astype(jnp.int32)
    e2 = route[:, R_E2].astype(jnp.int32)
    pos1 = jnp.take(offsets, e1) + route[:, R_RANK1].astype(jnp.int32)
    pos2 = jnp.take(offsets, e2) + route[:, R_RANK2].astype(jnp.int32)
    return pos1, pos2


def kernel(x_prompt, x_sample, p_prompt, p_sample, g_attn, w_in, g_out_a, g_out_b, sink, w_o, g_ffn, w_group, b_group, w_router, b_router, w1, w3, w2, g_ple, w_ple_gate, b_ple_gate, w_ple_proj, g_final):
    depth = w_in.shape[0]
    assert depth == 1
    l = 0
    D = w_in.shape[1]
    pad = LANES - N_EXPERTS - N_GROUPS
    w_route = jnp.pad(jnp.concatenate([w_router[l], w_group[l]], axis=-1), ((0, 0), (0, pad)))
    w_route_hi = w_route.astype(BF16)
    w_route_lo = (w_route - w_route_hi.astype(F32)).astype(BF16)
    wts = {
        "g_attn": g_attn[l].reshape(1, D),
        "w_in": w_in[l].astype(BF16),
        "g_out_a": g_out_a[l].reshape(1, A_WIDTH),
        "g_out_b": g_out_b[l].reshape(1, B_WIDTH),
        "sink": sink[l],
        "w_o": w_o[l].astype(BF16),
        "g_ffn": g_ffn[l].reshape(1, D),
        "w_route": jnp.concatenate([w_route_hi, w_route_lo], axis=-1),
        "b_route": jnp.pad(jnp.concatenate([b_router[l], b_group[l]]), (0, pad)).reshape(1, LANES),
        "w1": w1[l].astype(BF16),
        "w3": w3[l].astype(BF16),
        "w2": w2[l].astype(BF16),
        "g_ple": g_ple[l].reshape(1, D),
        "w_ple_gate": w_ple_gate[l].astype(BF16),
        "b_ple_gate": b_ple_gate[l].reshape(1, D),
        "w_ple_proj": w_ple_proj[l].astype(BF16),
        "g_final": g_final.reshape(1, D),
    }
    xs_list = (x_prompt, x_sample)
    ps_list = (p_prompt[l], p_sample[l])

    counts = jnp.zeros((1, LANES), F32)
    mixed = []
    for x in xs_list:
        h1, u2, route, counts = _mixer(x, wts, counts)
        mixed.append((h1, u2, route))

    n_pairs = 2 * sum(x.shape[0] * x.shape[1] for x in xs_list)
    n_tiles = n_pairs // MOE_TILE + N_EXPERTS
    offsets, tile_expert, n_used, fill_starts = _sorted_layout(counts, n_tiles, MOE_TILE)
    pos = [_positions(route, offsets) for _, _, route in mixed]
    pos_tiles = jnp.concatenate([_tile_positions(p1, p2, DISPATCH_TILE) for p1, p2 in pos], axis=0)
    xs = _dispatch([u2 for _, u2, _ in mixed], pos_tiles, fill_starts, n_tiles * MOE_TILE,
                   DISPATCH_TILE, MOE_TILE)
    ys = _experts(xs, tile_expert, n_used, wts["w1"], wts["w3"], wts["w2"], MOE_TILE)

    outs = []
    for x, p, (h1, u2, route), (pos1, pos2) in zip(xs_list, ps_list, mixed, pos):
        y = _ple(_tile_positions(pos1, pos2, PLE_TILE), ys, h1, route, p.reshape(h1.shape[0], -1),
                 wts["g_ple"], wts["w_ple_gate"], wts["b_ple_gate"], wts["w_ple_proj"], wts["g_final"],
                 PLE_TILE)
        outs.append(y.reshape(x.shape))
    return tuple(outs)
```

```python
import functools

import jax
import jax.numpy as jnp
import numpy as np
from jax import lax
from jax.experimental import pallas as pl
from jax.experimental.pallas import tpu as pltpu

HEAD_DIM = 128
A_HEADS = 8
A_WIDTH = A_HEADS * HEAD_DIM
B_HEADS = 8
B_KV_HEADS = 2
B_GROUP = B_HEADS // B_KV_HEADS
B_WIDTH = B_HEADS * HEAD_DIM
B_KV_WIDTH = B_KV_HEADS * HEAD_DIM
DILATED_CONFIGS = ((128, 1), (512, 4), (2048, 16))
B_RADIUS = 128
ROPE_THETA = 500000.0
ROPE_DIM = HEAD_DIM // 4
N_GROUPS = 4
EXPERTS_PER_GROUP = 8
N_EXPERTS = N_GROUPS * EXPERTS_PER_GROUP
RMS_EPS = 1e-6
NEG_INF = -1e30

LANES = 128
SUBLANES = 8
VMEM_LIMIT = 56 * 1024 * 1024

F32 = jnp.float32
BF16 = jnp.bfloat16


def _resident(shape):
    nd = len(shape)
    return pl.BlockSpec(shape, lambda *_: (0,) * nd, pipeline_mode=pl.Buffered(1))


def _params(sem):
    return pltpu.CompilerParams(dimension_semantics=sem, vmem_limit_bytes=VMEM_LIMIT)


def _rms(x, g):
    return x * lax.rsqrt(jnp.mean(x * x, axis=-1, keepdims=True) + RMS_EPS) * g


STRIDED_DILATIONS = tuple(d for _, d in DILATED_CONFIGS if d > 1)


def _qkv_kernel(x_ref, g_ref, w_ref, cos_ref, sa_ref, sb_ref, *rest):
    n_dil = len(STRIDED_DILATIONS)
    perm_refs = rest[:n_dil]
    qa_ref, ka_ref, va_ref, qb_ref, kb_ref, vb_ref = rest[n_dil:n_dil + 6]
    dil_refs = [rest[n_dil + 6 + i * n_dil:n_dil + 6 + (i + 1) * n_dil] for i in range(3)]
    tm = x_ref.shape[0]
    u = _rms(x_ref[...], g_ref[...]).astype(BF16)
    cos = cos_ref[...]
    sa = sa_ref[...]
    sb = sb_ref[...]
    half = ROPE_DIM // 2
    scale = HEAD_DIM ** -0.5

    def rope(t):
        return t * cos + pltpu.roll(t, half, 1) * sa + pltpu.roll(t, LANES - half, 1) * sb

    sections = (
        (qa_ref, 0, A_HEADS, True, scale, 0),
        (ka_ref, A_WIDTH, A_HEADS, True, None, 1),
        (va_ref, 2 * A_WIDTH, A_HEADS, False, None, 2),
        (qb_ref, 3 * A_WIDTH, B_HEADS, True, scale, None),
        (kb_ref, 3 * A_WIDTH + B_WIDTH, B_KV_HEADS, True, None, None),
        (vb_ref, 3 * A_WIDTH + B_WIDTH + B_KV_WIDTH, B_KV_HEADS, False, None, None),
    )
    for out_ref, c0, heads, rotary, sc, a_idx in sections:
        for h0 in range(0, heads, 4):
            nh = min(4, heads - h0)
            lo = c0 + h0 * HEAD_DIM
            acc = jnp.dot(u, w_ref[:, lo:lo + nh * HEAD_DIM], preferred_element_type=F32)
            for j in range(nh):
                t = acc[:, j * HEAD_DIM:(j + 1) * HEAD_DIM]
                if rotary:
                    t = rope(t)
                if sc is not None:
                    t = t * sc
                cols = slice((h0 + j) * HEAD_DIM, (h0 + j + 1) * HEAD_DIM)
                out_ref[:, cols] = t.astype(BF16)
        if a_idx is not None:
            for d, perm_ref, dref in zip(STRIDED_DILATIONS, perm_refs, dil_refs[a_idx]):
                n = tm // d
                permuted = jnp.dot(perm_ref[...], out_ref[...], preferred_element_type=F32).astype(BF16)
                for r in range(d):
                    dref[:, r * A_WIDTH:(r + 1) * A_WIDTH] = permuted[r * n:(r + 1) * n, :]


def _qkv(x2d, g, w_in, cos, sa, sb, seq, tm):
    T, D = x2d.shape
    n_pos = seq // tm
    row = lambda i: (i, 0)
    tab = pl.BlockSpec((tm, LANES), lambda i: (i % n_pos, 0))
    widths = (A_WIDTH, A_WIDTH, A_WIDTH, B_WIDTH, B_KV_WIDTH, B_KV_WIDTH)
    out_specs = [pl.BlockSpec((tm, w), row) for w in widths]
    out_shape = [jax.ShapeDtypeStruct((T, w), BF16) for w in widths]
    for _ in range(3):
        for d in STRIDED_DILATIONS:
            out_specs.append(pl.BlockSpec((tm // d, d * A_WIDTH), row))
            out_shape.append(jax.ShapeDtypeStruct((T // d, d * A_WIDTH), BF16))
    perms = [_residue_permutation(tm, d) for d in STRIDED_DILATIONS]
    return pl.pallas_call(
        _qkv_kernel,
        grid=(T // tm,),
        in_specs=[pl.BlockSpec((tm, D), row), _resident((1, D)), _resident(w_in.shape), tab, tab, tab,
                  *[_resident((tm, tm)) for _ in perms]],
        out_specs=out_specs,
        out_shape=out_shape,
        compiler_params=_params(("parallel",)),
        name="qkv",
    )(x2d, g, w_in, cos, sa, sb, *perms)


def _residue_permutation(n, d):
    dst = np.arange(n)
    src = (dst % (n // d)) * d + dst // (n // d)
    p = np.zeros((n, n), np.float32)
    p[dst, src] = 1.0
    return jnp.asarray(p, BF16)


ATTN_SUB = 128
ATTN_BLOCK = 1024


def _band_mask(first_q, sub, radius, length):
    shape = (sub, sub + 2 * radius)
    row = lax.broadcasted_iota(jnp.int32, shape, 0)
    col = lax.broadcasted_iota(jnp.int32, shape, 1)
    kpos = first_q - radius + col
    return (jnp.abs(col - radius - row) <= radius) & (kpos >= 0) & (kpos < length)


def _fill_window(ext_ref, prev_ref, cur_ref, next_ref, radius):
    n = cur_ref.shape[0]
    ext_ref[0:radius, :] = prev_ref[...]
    ext_ref[radius:radius + n, :] = cur_ref[...]
    ext_ref[radius + n:, :] = next_ref[...]


_NT = (((1,), (1,)), ((), ()))


def _attn_a_kernel(q_ref, kp_ref, kc_ref, kn_ref, vp_ref, vc_ref, vn_ref, o_ref, lse_ref, kx_ref, vx_ref,
                   *, length, lq, radius, n_res):
    sub = min(ATTN_SUB, lq)
    _fill_window(kx_ref, kp_ref, kc_ref, kn_ref, radius)
    _fill_window(vx_ref, vp_ref, vc_ref, vn_ref, radius)
    lane = lax.broadcasted_iota(jnp.int32, (sub, LANES), 1)
    for j in range(lq // sub):
        valid = _band_mask(pl.program_id(2) * lq + j * sub, sub, radius, length)
        q_rows = slice(j * sub, (j + 1) * sub)
        k_rows = slice(j * sub, (j + 1) * sub + 2 * radius)
        for res in range(n_res):
            lse_all = jnp.zeros((sub, LANES), F32)
            for h in range(A_HEADS):
                sl = slice(res * A_WIDTH + h * HEAD_DIM, res * A_WIDTH + (h + 1) * HEAD_DIM)
                s = lax.dot_general(q_ref[q_rows, sl], kx_ref[k_rows, sl], _NT, preferred_element_type=F32)
                s = jnp.where(valid, s, NEG_INF)
                m = jnp.max(s, axis=-1, keepdims=True)
                p = jnp.exp(s - m)
                l = jnp.sum(p, axis=-1, keepdims=True)
                o = jnp.dot(p.astype(BF16), vx_ref[k_rows, sl], preferred_element_type=F32)
                o_ref[q_rows, sl] = (o / l).astype(o_ref.dtype)
                lse_all = jnp.where(lane == h, m + jnp.log(l), lse_all)
            lse_ref[q_rows, res * LANES:(res + 1) * LANES] = lse_all


def _attn_a(qa, ka, va, batch, seq, dil, radius):
    ls = seq // dil
    lq = min(ATTN_BLOCK, ls)
    n_res = min(dil, ATTN_BLOCK // lq)
    per = lq // radius
    n_halo = ls // radius
    view = lambda t: t.reshape(batch, ls, dil * A_WIDTH)
    w = n_res * A_WIDTH
    cur = pl.BlockSpec((None, lq, w), lambda b, r, i: (b, i, r))
    prv = pl.BlockSpec((None, radius, w), lambda b, r, i: (b, jnp.maximum(i * per - 1, 0), r))
    nxt = pl.BlockSpec((None, radius, w), lambda b, r, i: (b, jnp.minimum((i + 1) * per, n_halo - 1), r))
    o, lse = pl.pallas_call(
        functools.partial(_attn_a_kernel, length=ls, lq=lq, radius=radius, n_res=n_res),
        grid=(batch, dil // n_res, ls // lq),
        in_specs=[cur, prv, cur, nxt, prv, cur, nxt],
        out_specs=[cur, pl.BlockSpec((None, lq, n_res * LANES), lambda b, r, i: (b, i, r))],
        out_shape=[jax.ShapeDtypeStruct((batch, ls, dil * A_WIDTH), BF16),
                   jax.ShapeDtypeStruct((batch, ls, dil * LANES), F32)],
        scratch_shapes=[pltpu.VMEM((lq + 2 * radius, w), BF16), pltpu.VMEM((lq + 2 * radius, w), BF16)],
        compiler_params=_params(("parallel", "parallel", "parallel")),
        name=f"attn_a_d{dil}",
    )(view(qa), view(ka), view(ka), view(ka), view(va), view(va), view(va))
    return o.reshape(batch * ls, dil * A_WIDTH), lse.reshape(batch * ls, dil * LANES)


def _attn_b_kernel(sink_ref, q_ref, kp_ref, kc_ref, kn_ref, vp_ref, vc_ref, vn_ref, g_ref, o_ref,
                   kx_ref, vx_ref, *, length, lq, radius):
    sub = ATTN_SUB
    _fill_window(kx_ref, kp_ref, kc_ref, kn_ref, radius)
    _fill_window(vx_ref, vp_ref, vc_ref, vn_ref, radius)
    for j in range(lq // sub):
        valid = _band_mask(pl.program_id(1) * lq + j * sub, sub, radius, length)
        q_rows = slice(j * sub, (j + 1) * sub)
        k_rows = slice(j * sub, (j + 1) * sub + 2 * radius)
        outs = []
        for h in range(B_HEADS):
            kv = slice((h // B_GROUP) * HEAD_DIM, (h // B_GROUP + 1) * HEAD_DIM)
            s = lax.dot_general(q_ref[q_rows, h * HEAD_DIM:(h + 1) * HEAD_DIM], kx_ref[k_rows, kv], _NT,
                                preferred_element_type=F32)
            s = jnp.where(valid, s, NEG_INF)
            sink = sink_ref[h]
            m = jnp.maximum(jnp.max(s, axis=-1, keepdims=True), sink)
            p = jnp.exp(s - m)
            l = jnp.sum(p, axis=-1, keepdims=True) + jnp.exp(sink - m)
            outs.append(jnp.dot(p.astype(BF16), vx_ref[k_rows, kv], preferred_element_type=F32) / l)
        o_ref[q_rows, :] = _rms(jnp.concatenate(outs, axis=-1), g_ref[...]).astype(o_ref.dtype)


def _attn_b(qb, kb, vb, sink, g_out_b, batch, seq):
    radius = B_RADIUS
    lq = min(ATTN_BLOCK, seq)
    per = lq // radius
    n_halo = seq // radius
    q3 = qb.reshape(batch, seq, B_WIDTH)
    k3 = kb.reshape(batch, seq, B_KV_WIDTH)
    v3 = vb.reshape(batch, seq, B_KV_WIDTH)
    qspec = pl.BlockSpec((None, lq, B_WIDTH), lambda b, i: (b, i, 0))
    cur = pl.BlockSpec((None, lq, B_KV_WIDTH), lambda b, i: (b, i, 0))
    prv = pl.BlockSpec((None, radius, B_KV_WIDTH), lambda b, i: (b, jnp.maximum(i * per - 1, 0), 0))
    nxt = pl.BlockSpec((None, radius, B_KV_WIDTH),
                       lambda b, i: (b, jnp.minimum((i + 1) * per, n_halo - 1), 0))
    ext = pltpu.VMEM((lq + 2 * radius, B_KV_WIDTH), BF16)
    out = pl.pallas_call(
        functools.partial(_attn_b_kernel, length=seq, lq=lq, radius=radius),
        grid=(batch, seq // lq),
        in_specs=[pl.BlockSpec(memory_space=pltpu.SMEM), qspec, prv, cur, nxt, prv, cur, nxt,
                  _resident((1, B_WIDTH))],
        out_specs=qspec,
        out_shape=jax.ShapeDtypeStruct((batch, seq, B_WIDTH), BF16),
        scratch_shapes=[ext, ext],
        compiler_params=_params(("parallel", "parallel")),
        name="attn_b",
    )(sink, q3, k3, k3, k3, v3, v3, v3, g_out_b)
    return out.reshape(batch * seq, B_WIDTH)


R_E1, R_E2, R_W1, R_W2, R_RANK1, R_RANK2 = range(6)


def _oproj_kernel(o1_ref, o4_ref, o16_ref, l1_ref, l4_ref, l16_ref, mb_ref, x_ref, wo_ref,
                  ga_ref, gf_ref, wr_ref, br_ref, cnt_in_ref, h_ref, u_ref, route_ref, route_t_ref, cnt_ref,
                  o4n_ref, o16n_ref, l4n_ref, l16n_ref, *, sub):
    tm = x_ref.shape[0]

    @pl.when(pl.program_id(0) == 0)
    def _():
        cnt_ref[...] = cnt_in_ref[...]

    for d, o_src, o_dst, l_src, l_dst in ((STRIDED_DILATIONS[0], o4_ref, o4n_ref, l4_ref, l4n_ref),
                                          (STRIDED_DILATIONS[1], o16_ref, o16n_ref, l16_ref, l16n_ref)):
        for r in range(d):
            rows = pl.ds(r, tm // d, stride=d)
            for h in range(A_HEADS):
                lo = r * A_WIDTH + h * HEAD_DIM
                o_dst[h, rows, :] = o_src[:, lo:lo + HEAD_DIM].astype(F32)
            l_dst[rows, :] = l_src[:, r * LANES:(r + 1) * LANES]

    for c in range(tm // sub):
        _oproj_rows(slice(c * sub, (c + 1) * sub), o1_ref, o4n_ref, o16n_ref, l1_ref, l4n_ref, l16n_ref,
                    mb_ref, x_ref, wo_ref, ga_ref, gf_ref, wr_ref, br_ref, h_ref, u_ref, route_ref,
                    route_t_ref, cnt_ref)


def _oproj_rows(rs, o1_ref, o2_ref, o3_ref, l1_ref, l2_ref, l3_ref, mb_ref, x_ref, wo_ref,
                ga_ref, gf_ref, wr_ref, br_ref, h_ref, u_ref, route_ref, route_t_ref, cnt_ref):
    tm = rs.stop - rs.start
    l1, l2, l3 = l1_ref[rs, :], l2_ref[rs, :], l3_ref[rs, :]
    mx = jnp.maximum(jnp.maximum(l1, l2), l3)
    e1, e2, e3 = jnp.exp(l1 - mx), jnp.exp(l2 - mx), jnp.exp(l3 - mx)
    inv = 1.0 / (e1 + e2 + e3)
    w1, w2, w3 = e1 * inv, e2 * inv, e3 * inv
    parts = []
    for h in range(A_HEADS):
        sl = slice(h * HEAD_DIM, (h + 1) * HEAD_DIM)
        bc = lambda w: jnp.broadcast_to(w[:, h:h + 1], (tm, HEAD_DIM))
        parts.append(bc(w1) * o1_ref[rs, sl].astype(F32) + bc(w2) * o2_ref[h, rs, :]
                     + bc(w3) * o3_ref[h, rs, :])
    mixed_a = _rms(jnp.concatenate(parts, axis=-1), ga_ref[...]).astype(BF16)
    h = (x_ref[rs, :]
         + jnp.dot(mixed_a, wo_ref[:A_WIDTH, :], preferred_element_type=F32)
         + jnp.dot(mb_ref[rs, :], wo_ref[A_WIDTH:, :], preferred_element_type=F32))
    h_ref[rs, :] = h
    u = _rms(h, gf_ref[...])
    u_ref[rs, :] = u

    u_hi = u.astype(BF16)
    u_lo = (u - u_hi.astype(F32)).astype(BF16)
    z = (jnp.dot(u_hi, wr_ref[...], preferred_element_type=F32)
         + jnp.dot(u_lo, wr_ref[...], preferred_element_type=F32))
    logits = z[:, :LANES] + z[:, LANES:] + br_ref[...]
    lane = lax.broadcasted_iota(jnp.int32, (tm, LANES), 1)
    big = jnp.int32(LANES)
    is_group = (lane >= N_EXPERTS) & (lane < N_EXPERTS + N_GROUPS)
    gmax = jnp.max(jnp.where(is_group, logits, -jnp.inf), axis=-1, keepdims=True)
    gidx = jnp.min(jnp.where(is_group & (logits == gmax), lane - N_EXPERTS, big), axis=-1, keepdims=True)
    ggate = 1.0 / jnp.sum(jnp.where(is_group, jnp.exp(logits - gmax), 0.0), axis=-1, keepdims=True)
    in_group = (lane >= gidx * EXPERTS_PER_GROUP) & (lane < (gidx + 1) * EXPERTS_PER_GROUP)
    v1 = jnp.max(jnp.where(in_group, logits, -jnp.inf), axis=-1, keepdims=True)
    i1 = jnp.min(jnp.where(in_group & (logits == v1), lane, big), axis=-1, keepdims=True)
    rest = in_group & (lane != i1)
    v2 = jnp.max(jnp.where(rest, logits, -jnp.inf), axis=-1, keepdims=True)
    i2 = jnp.min(jnp.where(rest & (logits == v2), lane, big), axis=-1, keepdims=True)
    t = jnp.exp(v2 - v1)
    tw1 = ggate / (1.0 + t)
    tw2 = ggate * t / (1.0 + t)

    onehot = ((lane == i1) | (lane == i2)).astype(F32)
    r_i = lax.broadcasted_iota(jnp.int32, (tm, tm), 0)
    c_i = lax.broadcasted_iota(jnp.int32, (tm, tm), 1)
    before = (c_i < r_i).astype(BF16)
    prefix = jnp.dot(before, onehot.astype(BF16), preferred_element_type=F32) + cnt_ref[...]
    rank1 = jnp.sum(jnp.where(lane == i1, prefix, 0.0), axis=-1, keepdims=True)
    rank2 = jnp.sum(jnp.where(lane == i2, prefix, 0.0), axis=-1, keepdims=True)
    cnt_ref[...] += jnp.sum(onehot, axis=0, keepdims=True)

    rec = jnp.zeros((tm, LANES), F32)
    for slot, val in ((R_E1, i1.astype(F32)), (R_E2, i2.astype(F32)), (R_W1, tw1), (R_W2, tw2),
                      (R_RANK1, rank1), (R_RANK2, rank2)):
        rec = jnp.where(lane == slot, val, rec)
    route_ref[rs, :] = rec
    route_t_ref[:, rs] = rec.T[:SUBLANES, :]


def _oproj(o_list, lse_list, mixed_b, x2d, w_o, g_out_a, g_ffn, w_route, b_route, cnt_in, tm, sub):
    T, D = x2d.shape
    row = lambda i: (i, 0)
    dils = (1,) + STRIDED_DILATIONS
    aspecs = [pl.BlockSpec((tm // d, d * A_WIDTH), row) for d in dils]
    lspecs = [pl.BlockSpec((tm // d, d * LANES), row) for d in dils]
    lspec = pl.BlockSpec((tm, LANES), row)
    dspec = pl.BlockSpec((tm, D), row)
    cspec = pl.BlockSpec((1, LANES), lambda i: (0, 0))
    return pl.pallas_call(
        functools.partial(_oproj_kernel, sub=sub),
        grid=(T // tm,),
        in_specs=[*aspecs, *lspecs, pl.BlockSpec((tm, B_WIDTH), row), dspec,
                  _resident(w_o.shape), _resident((1, A_WIDTH)), _resident((1, D)),
                  _resident(w_route.shape), _resident((1, LANES)), cspec],
        out_specs=[dspec, dspec, lspec, pl.BlockSpec((SUBLANES, tm), lambda i: (0, i)), cspec],
        out_shape=[jax.ShapeDtypeStruct((T, D), F32), jax.ShapeDtypeStruct((T, D), F32),
                   jax.ShapeDtypeStruct((T, LANES), F32), jax.ShapeDtypeStruct((SUBLANES, T), F32),
                   jax.ShapeDtypeStruct((1, LANES), F32)],
        scratch_shapes=[pltpu.VMEM((A_HEADS, tm, HEAD_DIM), F32), pltpu.VMEM((A_HEADS, tm, HEAD_DIM), F32),
                        pltpu.VMEM((tm, LANES), F32), pltpu.VMEM((tm, LANES), F32)],
        compiler_params=_params(("arbitrary",)),
        name="oproj_route",
    )(*o_list, *lse_list, mixed_b, x2d, w_o, g_out_a, g_ffn, w_route, b_route, cnt_in)


def _row_copy(src_ref, src_row, dst_ref, dst_row, sem):
    return pltpu.make_async_copy(src_ref.at[pl.ds(src_row, 1)], dst_ref.at[pl.ds(dst_row, 1)], sem)


def _dispatch_kernel(pos_ref, fill_ref, *rest, tm, tiles_per_src, fill_rows):
    n_src = len(tiles_per_src)
    u_refs = rest[:n_src]
    xs_ref, sem, zero_ref, fill_sem = rest[n_src:]
    i = pl.program_id(0)

    @pl.when(i == 0)
    def _():
        zero_ref[...] = jnp.zeros_like(zero_ref)

        def tile_copy(k):
            start = pl.multiple_of(fill_ref[k], fill_rows)
            return pltpu.make_async_copy(zero_ref, xs_ref.at[pl.ds(start, fill_rows)], fill_sem)

        for k in range(fill_ref.shape[0]):
            @pl.when(fill_ref[k] >= 0)
            def _():
                tile_copy(k).start()

        for k in range(fill_ref.shape[0]):
            @pl.when(fill_ref[k] >= 0)
            def _():
                tile_copy(k).wait()

    def scatter(u_ref):
        def issue(j, carry):
            _row_copy(u_ref, j, xs_ref, pos_ref[0, j], sem).start()
            _row_copy(u_ref, j, xs_ref, pos_ref[0, tm + j], sem).start()
            return carry

        lax.fori_loop(0, tm, issue, 0, unroll=8)

        def drain(j, carry):
            _row_copy(u_ref, 0, xs_ref, 0, sem).wait()
            return carry

        lax.fori_loop(0, 2 * tm, drain, 0, unroll=8)

    first = 0
    for u_ref, n in zip(u_refs, tiles_per_src):
        @pl.when((i >= first) & (i < first + n))
        def _():
            scatter(u_ref)

        first += n


def _tile_positions(pos1, pos2, tm):
    n = pos1.shape[0] // tm
    return jnp.stack([pos1.reshape(n, tm), pos2.reshape(n, tm)], axis=1).reshape(n, 1, 2 * tm)


def _dispatch(us, pos_tiles, fill_starts, n_rows, tm, fill_rows):
    D = us[0].shape[1]
    tiles_per_src = tuple(u.shape[0] // tm for u in us)
    in_specs = [pl.BlockSpec((None, 1, 2 * tm), lambda i: (i, 0, 0), memory_space=pltpu.SMEM),
                pl.BlockSpec(memory_space=pltpu.SMEM)]
    first = 0
    for n in tiles_per_src:
        in_specs.append(pl.BlockSpec((tm, D), lambda i, first=first, n=n: (jnp.clip(i - first, 0, n - 1), 0)))
        first += n
    return pl.pallas_call(
        functools.partial(_dispatch_kernel, tm=tm, tiles_per_src=tiles_per_src, fill_rows=fill_rows),
        grid=(first,),
        in_specs=in_specs,
        out_specs=pl.BlockSpec(memory_space=pl.ANY),
        out_shape=jax.ShapeDtypeStruct((n_rows, D), F32),
        scratch_shapes=[pltpu.SemaphoreType.DMA(()), pltpu.VMEM((fill_rows, D), F32),
                        pltpu.SemaphoreType.DMA(())],
        compiler_params=_params(("arbitrary",)),
        name="moe_dispatch",
    )(pos_tiles, fill_starts, *us)


def _experts_kernel(te_ref, used_ref, x_ref, w1_ref, w3_ref, w2_ref, y_ref, w1b_ref, w3b_ref, w2b_ref):
    i = pl.program_id(0)
    cur = jnp.minimum(i, used_ref[0] - 1)
    new_expert = (i == 0) | (te_ref[cur] != te_ref[jnp.maximum(cur - 1, 0)])

    @pl.when(new_expert & (i < used_ref[0]))
    def _():
        w1b_ref[...] = w1_ref[...].astype(BF16)
        w3b_ref[...] = w3_ref[...].astype(BF16)
        w2b_ref[...] = w2_ref[...].astype(BF16)

    @pl.when(i < used_ref[0])
    def _():
        x = x_ref[...].astype(BF16)
        a = jnp.dot(x, w1b_ref[...], preferred_element_type=F32)
        b = jnp.dot(x, w3b_ref[...], preferred_element_type=F32)
        hid = (a * jax.nn.sigmoid(a)) * b
        y_ref[...] = jnp.dot(hid.astype(BF16), w2b_ref[...], preferred_element_type=F32)

    @pl.when(i >= used_ref[0])
    def _():
        y_ref[...] = jnp.zeros_like(y_ref)


def _experts(xs, tile_expert, n_used, w1, w3, w2, tm):
    P, D = xs.shape
    F = w1.shape[2]
    row = lambda i, te, used: (jnp.minimum(i, used[0] - 1), 0)
    wmap = lambda i, te, used: (te[jnp.minimum(i, used[0] - 1)], 0, 0)
    grid_spec = pltpu.PrefetchScalarGridSpec(
        num_scalar_prefetch=2,
        grid=(P // tm,),
        in_specs=[pl.BlockSpec((tm, D), row),
                  pl.BlockSpec((None, D, F), wmap),
                  pl.BlockSpec((None, D, F), wmap),
                  pl.BlockSpec((None, F, D), wmap)],
        out_specs=pl.BlockSpec((tm, D), lambda i, te, used: (i, 0)),
        scratch_shapes=[pltpu.VMEM((D, F), BF16), pltpu.VMEM((D, F), BF16), pltpu.VMEM((F, D), BF16)],
    )
    return pl.pallas_call(
        _experts_kernel,
        grid_spec=grid_spec,
        out_shape=jax.ShapeDtypeStruct((P, D), F32),
        compiler_params=_params(("arbitrary",)),
        name="moe_experts",
    )(tile_expert, n_used, xs, w1, w3, w2)


def _ple_kernel(pos_ref, pos_next_ref, ys_ref, h_ref, route_ref, p_ref, gp_ref, wg_ref, bg_ref, wp_ref,
                gf_ref, o_ref, ybuf, sems, *, tm):
    i = pl.program_id(0)
    slot = i % 2

    def gather(p_ref_, s):
        def issue(j, carry):
            _row_copy(ys_ref, p_ref_[0, j], ybuf.at[s, 0], j, sems.at[s]).start()
            _row_copy(ys_ref, p_ref_[0, tm + j], ybuf.at[s, 1], j, sems.at[s]).start()
            return carry

        lax.fori_loop(0, tm, issue, 0, unroll=8)

    @pl.when(i == 0)
    def _():
        gather(pos_ref, 0)

    @pl.when(i + 1 < pl.num_programs(0))
    def _():
        gather(pos_next_ref, 1 - slot)

    def drain(j, carry):
        _row_copy(ys_ref, 0, ybuf.at[slot, 0], 0, sems.at[slot]).wait()
        return carry

    lax.fori_loop(0, 2 * tm, drain, 0, unroll=8)

    route = route_ref[...]
    h = (h_ref[...] + route[:, R_W1:R_W1 + 1] * ybuf[slot, 0] + route[:, R_W2:R_W2 + 1] * ybuf[slot, 1])
    u = _rms(h, gp_ref[...]).astype(BF16)
    gate = jax.nn.sigmoid(jnp.dot(u, wg_ref[...], preferred_element_type=F32) + bg_ref[...])
    proj = jnp.dot(p_ref[...].astype(BF16), wp_ref[...], preferred_element_type=F32)
    o_ref[...] = _rms(h + gate * proj, gf_ref[...])


def _ple(pos_tiles, ys, h, route, p2d, g_ple, w_gate, b_gate, w_proj, g_final, tm):
    T, D = h.shape
    P = p2d.shape[1]
    row = lambda i: (i, 0)
    n = T // tm
    return pl.pallas_call(
        functools.partial(_ple_kernel, tm=tm),
        grid=(n,),
        in_specs=[pl.BlockSpec((None, 1, 2 * tm), lambda i: (i, 0, 0), memory_space=pltpu.SMEM),
                  pl.BlockSpec((None, 1, 2 * tm), lambda i: (jnp.minimum(i + 1, n - 1), 0, 0),
                               memory_space=pltpu.SMEM),
                  pl.BlockSpec(memory_space=pl.ANY),
                  pl.BlockSpec((tm, D), row), pl.BlockSpec((tm, LANES), row), pl.BlockSpec((tm, P), row),
                  _resident((1, D)), _resident(w_gate.shape), _resident((1, D)), _resident(w_proj.shape),
                  _resident((1, D))],
        out_specs=pl.BlockSpec((tm, D), row),
        out_shape=jax.ShapeDtypeStruct((T, D), F32),
        scratch_shapes=[pltpu.VMEM((2, 2, tm, D), F32), pltpu.SemaphoreType.DMA((2,))],
        compiler_params=_params(("arbitrary",)),
        name="ple_final",
    )(pos_tiles, pos_tiles, ys, h, route, p2d, g_ple, w_gate, b_gate, w_proj, g_final)


def _rope_tables(seq):
    half = ROPE_DIM // 2
    inv = ROPE_THETA ** (-(jnp.arange(half, dtype=F32) * 2.0) / ROPE_DIM)
    ang = jnp.arange(seq, dtype=F32)[:, None] * inv[None, :]
    cos, sin = jnp.cos(ang), jnp.sin(ang)
    zeros = jnp.zeros((seq, LANES - ROPE_DIM), F32)
    z16 = jnp.zeros((seq, half), F32)
    cos_t = jnp.concatenate([cos, cos, jnp.ones((seq, LANES - ROPE_DIM), F32)], axis=-1)
    sa_t = jnp.concatenate([z16, sin, zeros], axis=-1)
    sb_t = jnp.concatenate([-sin, z16, zeros], axis=-1)
    return cos_t, sa_t, sb_t


MOE_TILE = 256
OPROJ_TILE = 512
PLE_TILE = 256
DISPATCH_TILE = 1024


def _mixer(x, wts, cnt_in):
    batch, seq, D = x.shape
    T = batch * seq
    x2d = x.reshape(T, D)
    cos, sa, sb = _rope_tables(seq)
    qa, ka, va, qb, kb, vb, *strided = _qkv(x2d, wts["g_attn"], wts["w_in"], cos, sa, sb, seq, tm=256)
    n_dil = len(STRIDED_DILATIONS)
    qkv_by_dil = {1: (qa, ka, va)}
    for n, d in enumerate(STRIDED_DILATIONS):
        qkv_by_dil[d] = tuple(strided[i * n_dil + n] for i in range(3))
    o_list, lse_list = [], []
    for window, dil in DILATED_CONFIGS:
        o, lse = _attn_a(*qkv_by_dil[dil], batch, seq, dil, window // (2 * dil))
        o_list.append(o)
        lse_list.append(lse)
    mixed_b = _attn_b(qb, kb, vb, wts["sink"], wts["g_out_b"], batch, seq)
    return _oproj(o_list, lse_list, mixed_b, x2d, wts["w_o"], wts["g_out_a"], wts["g_ffn"],
                  wts["w_route"], wts["b_route"], cnt_in, tm=OPROJ_TILE, sub=OPROJ_TILE // 2)


def _sorted_layout(counts, n_tiles, tm):
    cnt = counts[0, :N_EXPERTS].astype(jnp.int32)
    tiles = (cnt + tm - 1) // tm
    ends = jnp.cumsum(tiles)
    offsets = (ends - tiles) * tm
    tile_expert = jnp.sum(jnp.arange(n_tiles, dtype=jnp.int32)[:, None] >= ends[None, :], axis=1)
    tile_expert = jnp.minimum(tile_expert, N_EXPERTS - 1).astype(jnp.int32)
    n_used = ends[-1]
    last_tile_start = jnp.where(tiles > 0, (ends - 1) * tm, -1)
    tail = n_used + jnp.arange(N_EXPERTS, dtype=jnp.int32)
    tail_start = jnp.where(tail < n_tiles, tail * tm, -1)
    fill_starts = jnp.concatenate([last_tile_start, tail_start]).astype(jnp.int32)
    return offsets, tile_expert, n_used.reshape(1).astype(jnp.int32), fill_starts


def _positions(route_t, offsets):
    e1 = route_t[R_E1].astype(jnp.int32)
    e2 = route_t[R_E2].astype(jnp.int32)
    pos1 = jnp.take(offsets, e1) + route_t[R_RANK1].astype(jnp.int32)
    pos2 = jnp.take(offsets, e2) + route_t[R_RANK2].astype(jnp.int32)
    return pos1, pos2


def kernel(x_prompt, x_sample, p_prompt, p_sample, g_attn, w_in, g_out_a, g_out_b, sink, w_o, g_ffn, w_group, b_group, w_router, b_router, w1, w3, w2, g_ple, w_ple_gate, b_ple_gate, w_ple_proj, g_final):
    depth = w_in.shape[0]
    assert depth == 1
    l = 0
    D = w_in.shape[1]
    pad = LANES - N_EXPERTS - N_GROUPS
    w_route = jnp.pad(jnp.concatenate([w_router[l], w_group[l]], axis=-1), ((0, 0), (0, pad)))
    w_route_hi = w_route.astype(BF16)
    w_route_lo = (w_route - w_route_hi.astype(F32)).astype(BF16)
    wts = {
        "g_attn": g_attn[l].reshape(1, D),
        "w_in": w_in[l].astype(BF16),
        "g_out_a": g_out_a[l].reshape(1, A_WIDTH),
        "g_out_b": g_out_b[l].reshape(1, B_WIDTH),
        "sink": sink[l],
        "w_o": w_o[l].astype(BF16),
        "g_ffn": g_ffn[l].reshape(1, D),
        "w_route": jnp.concatenate([w_route_hi, w_route_lo], axis=-1),
        "b_route": jnp.pad(jnp.concatenate([b_router[l], b_group[l]]), (0, pad)).reshape(1, LANES),
        "w1": w1[l],
        "w3": w3[l],
        "w2": w2[l],
        "g_ple": g_ple[l].reshape(1, D),
        "w_ple_gate": w_ple_gate[l].astype(BF16),
        "b_ple_gate": b_ple_gate[l].reshape(1, D),
        "w_ple_proj": w_ple_proj[l].astype(BF16),
        "g_final": g_final.reshape(1, D),
    }
    xs_list = (x_prompt, x_sample)
    ps_list = (p_prompt[l], p_sample[l])

    counts = jnp.zeros((1, LANES), F32)
    mixed = []
    route_ts = []
    for x in xs_list:
        h1, u2, route, route_t, counts = _mixer(x, wts, counts)
        mixed.append((h1, u2, route))
        route_ts.append(route_t)

    n_pairs = 2 * sum(x.shape[0] * x.shape[1] for x in xs_list)
    n_tiles = n_pairs // MOE_TILE + N_EXPERTS
    offsets, tile_expert, n_used, fill_starts = _sorted_layout(counts, n_tiles, MOE_TILE)
    pos = [_positions(route_t, offsets) for route_t in route_ts]
    pos_tiles = jnp.concatenate([_tile_positions(p1, p2, DISPATCH_TILE) for p1, p2 in pos], axis=0)
    xs = _dispatch([u2 for _, u2, _ in mixed], pos_tiles, fill_starts, n_tiles * MOE_TILE,
                   DISPATCH_TILE, MOE_TILE)
    ys = _experts(xs, tile_expert, n_used, wts["w1"], wts["w3"], wts["w2"], MOE_TILE)

    outs = []
    for x, p, (h1, u2, route), (pos1, pos2) in zip(xs_list, ps_list, mixed, pos):
        y = _ple(_tile_positions(pos1, pos2, PLE_TILE), ys, h1, route, p.reshape(h1.shape[0], -1),
                 wts["g_ple"], wts["w_ple_gate"], wts["b_ple_gate"], wts["w_ple_proj"], wts["g_final"],
                 PLE_TILE)
        outs.append(y.reshape(x.shape))
    return tuple(outs)
```

```python
import functools

import jax
import jax.numpy as jnp
import numpy as np
from jax import lax
from jax.experimental import pallas as pl
from jax.experimental.pallas import tpu as pltpu

HEAD_DIM = 128
A_HEADS = 8
A_WIDTH = A_HEADS * HEAD_DIM
B_HEADS = 8
B_KV_HEADS = 2
B_GROUP = B_HEADS // B_KV_HEADS
B_WIDTH = B_HEADS * HEAD_DIM
B_KV_WIDTH = B_KV_HEADS * HEAD_DIM
DILATED_CONFIGS = ((128, 1), (512, 4), (2048, 16))
B_RADIUS = 128
ROPE_THETA = 500000.0
ROPE_DIM = HEAD_DIM // 4
N_GROUPS = 4
EXPERTS_PER_GROUP = 8
N_EXPERTS = N_GROUPS * EXPERTS_PER_GROUP
RMS_EPS = 1e-6
NEG_INF = -1e30

LANES = 128
SUBLANES = 8
VMEM_LIMIT = 56 * 1024 * 1024

F32 = jnp.float32
BF16 = jnp.bfloat16


def _resident(shape):
    nd = len(shape)
    return pl.BlockSpec(shape, lambda *_: (0,) * nd, pipeline_mode=pl.Buffered(1))


def _params(sem):
    return pltpu.CompilerParams(dimension_semantics=sem, vmem_limit_bytes=VMEM_LIMIT)


def _rms(x, g):
    return x * lax.rsqrt(jnp.mean(x * x, axis=-1, keepdims=True) + RMS_EPS) * g


STRIDED_DILATIONS = tuple(d for _, d in DILATED_CONFIGS if d > 1)


def _qkv_kernel(x_ref, g_ref, w_ref, cos_ref, sa_ref, sb_ref, *rest):
    n_dil = len(STRIDED_DILATIONS)
    perm_refs = rest[:n_dil]
    qa_ref, ka_ref, va_ref, qb_ref, kb_ref, vb_ref = rest[n_dil:n_dil + 6]
    dil_refs = [rest[n_dil + 6 + i * n_dil:n_dil + 6 + (i + 1) * n_dil] for i in range(3)]
    tm = x_ref.shape[0]
    u = _rms(x_ref[...], g_ref[...]).astype(BF16)
    cos = cos_ref[...]
    sa = sa_ref[...]
    sb = sb_ref[...]
    half = ROPE_DIM // 2
    scale = HEAD_DIM ** -0.5

    def rope(t):
        return t * cos + pltpu.roll(t, half, 1) * sa + pltpu.roll(t, LANES - half, 1) * sb

    sections = (
        (qa_ref, 0, A_HEADS, True, scale, 0),
        (ka_ref, A_WIDTH, A_HEADS, True, None, 1),
        (va_ref, 2 * A_WIDTH, A_HEADS, False, None, 2),
        (qb_ref, 3 * A_WIDTH, B_HEADS, True, scale, None),
        (kb_ref, 3 * A_WIDTH + B_WIDTH, B_KV_HEADS, True, None, None),
        (vb_ref, 3 * A_WIDTH + B_WIDTH + B_KV_WIDTH, B_KV_HEADS, False, None, None),
    )
    for out_ref, c0, heads, rotary, sc, a_idx in sections:
        for h0 in range(0, heads, 4):
            nh = min(4, heads - h0)
            lo = c0 + h0 * HEAD_DIM
            acc = jnp.dot(u, w_ref[:, lo:lo + nh * HEAD_DIM], preferred_element_type=F32)
            for j in range(nh):
                t = acc[:, j * HEAD_DIM:(j + 1) * HEAD_DIM]
                if rotary:
                    t = rope(t)
                if sc is not None:
                    t = t * sc
                cols = slice((h0 + j) * HEAD_DIM, (h0 + j + 1) * HEAD_DIM)
                out_ref[:, cols] = t.astype(BF16)
        if a_idx is not None:
            for d, perm_ref, dref in zip(STRIDED_DILATIONS, perm_refs, dil_refs[a_idx]):
                n = tm // d
                permuted = jnp.dot(perm_ref[...], out_ref[...], preferred_element_type=F32).astype(BF16)
                for r in range(d):
                    dref[:, r * A_WIDTH:(r + 1) * A_WIDTH] = permuted[r * n:(r + 1) * n, :]


def _qkv(x2d, g, w_in, cos, sa, sb, seq, tm):
    T, D = x2d.shape
    n_pos = seq // tm
    row = lambda i: (i, 0)
    tab = pl.BlockSpec((tm, LANES), lambda i: (i % n_pos, 0))
    widths = (A_WIDTH, A_WIDTH, A_WIDTH, B_WIDTH, B_KV_WIDTH, B_KV_WIDTH)
    out_specs = [pl.BlockSpec((tm, w), row) for w in widths]
    out_shape = [jax.ShapeDtypeStruct((T, w), BF16) for w in widths]
    for _ in range(3):
        for d in STRIDED_DILATIONS:
            out_specs.append(pl.BlockSpec((tm // d, d * A_WIDTH), row))
            out_shape.append(jax.ShapeDtypeStruct((T // d, d * A_WIDTH), BF16))
    perms = [_residue_permutation(tm, d) for d in STRIDED_DILATIONS]
    return pl.pallas_call(
        _qkv_kernel,
        grid=(T // tm,),
        in_specs=[pl.BlockSpec((tm, D), row), _resident((1, D)), _resident(w_in.shape), tab, tab, tab,
                  *[_resident((tm, tm)) for _ in perms]],
        out_specs=out_specs,
        out_shape=out_shape,
        compiler_params=_params(("parallel",)),
        name="qkv",
    )(x2d, g, w_in, cos, sa, sb, *perms)


def _residue_permutation(n, d):
    dst = np.arange(n)
    src = (dst % (n // d)) * d + dst // (n // d)
    p = np.zeros((n, n), np.float32)
    p[dst, src] = 1.0
    return jnp.asarray(p, BF16)


ATTN_SUB = 128
ATTN_BLOCK = 1024


def _band_mask(first_q, sub, radius, length):
    shape = (sub, sub + 2 * radius)
    row = lax.broadcasted_iota(jnp.int32, shape, 0)
    col = lax.broadcasted_iota(jnp.int32, shape, 1)
    kpos = first_q - radius + col
    return (jnp.abs(col - radius - row) <= radius) & (kpos >= 0) & (kpos < length)


def _fill_window(ext_ref, prev_ref, cur_ref, next_ref, radius):
    n = cur_ref.shape[0]
    ext_ref[0:radius, :] = prev_ref[...]
    ext_ref[radius:radius + n, :] = cur_ref[...]
    ext_ref[radius + n:, :] = next_ref[...]


_NT = (((1,), (1,)), ((), ()))


def _attn_a_kernel(q_ref, kp_ref, kc_ref, kn_ref, vp_ref, vc_ref, vn_ref, o_ref, lse_ref, kx_ref, vx_ref,
                   *, length, lq, radius, n_res):
    sub = min(ATTN_SUB, lq)
    _fill_window(kx_ref, kp_ref, kc_ref, kn_ref, radius)
    _fill_window(vx_ref, vp_ref, vc_ref, vn_ref, radius)
    lane = lax.broadcasted_iota(jnp.int32, (sub, LANES), 1)
    for j in range(lq // sub):
        valid = _band_mask(pl.program_id(2) * lq + j * sub, sub, radius, length)
        q_rows = slice(j * sub, (j + 1) * sub)
        k_rows = slice(j * sub, (j + 1) * sub + 2 * radius)
        for res in range(n_res):
            lse_all = jnp.zeros((sub, LANES), F32)
            for h in range(A_HEADS):
                sl = slice(res * A_WIDTH + h * HEAD_DIM, res * A_WIDTH + (h + 1) * HEAD_DIM)
                s = lax.dot_general(q_ref[q_rows, sl], kx_ref[k_rows, sl], _NT, preferred_element_type=F32)
                s = jnp.where(valid, s, NEG_INF)
                m = jnp.max(s, axis=-1, keepdims=True)
                p = jnp.exp(s - m)
                l = jnp.sum(p, axis=-1, keepdims=True)
                o = jnp.dot(p.astype(BF16), vx_ref[k_rows, sl], preferred_element_type=F32)
                o_ref[q_rows, sl] = (o / l).astype(o_ref.dtype)
                lse_all = jnp.where(lane == h, m + jnp.log(l), lse_all)
            lse_ref[q_rows, res * LANES:(res + 1) * LANES] = lse_all


def _attn_a(qa, ka, va, batch, seq, dil, radius):
    ls = seq // dil
    lq = min(ATTN_BLOCK, ls)
    n_res = min(dil, ATTN_BLOCK // lq)
    per = lq // radius
    n_halo = ls // radius
    view = lambda t: t.reshape(batch, ls, dil * A_WIDTH)
    w = n_res * A_WIDTH
    cur = pl.BlockSpec((None, lq, w), lambda b, r, i: (b, i, r))
    prv = pl.BlockSpec((None, radius, w), lambda b, r, i: (b, jnp.maximum(i * per - 1, 0), r))
    nxt = pl.BlockSpec((None, radius, w), lambda b, r, i: (b, jnp.minimum((i + 1) * per, n_halo - 1), r))
    o, lse = pl.pallas_call(
        functools.partial(_attn_a_kernel, length=ls, lq=lq, radius=radius, n_res=n_res),
        grid=(batch, dil // n_res, ls // lq),
        in_specs=[cur, prv, cur, nxt, prv, cur, nxt],
        out_specs=[cur, pl.BlockSpec((None, lq, n_res * LANES), lambda b, r, i: (b, i, r))],
        out_shape=[jax.ShapeDtypeStruct((batch, ls, dil * A_WIDTH), BF16),
                   jax.ShapeDtypeStruct((batch, ls, dil * LANES), F32)],
        scratch_shapes=[pltpu.VMEM((lq + 2 * radius, w), BF16), pltpu.VMEM((lq + 2 * radius, w), BF16)],
        compiler_params=_params(("parallel", "parallel", "parallel")),
        name=f"attn_a_d{dil}",
    )(view(qa), view(ka), view(ka), view(ka), view(va), view(va), view(va))
    return o.reshape(batch * ls, dil * A_WIDTH), lse.reshape(batch * ls, dil * LANES)


def _attn_b_kernel(sink_ref, q_ref, kp_ref, kc_ref, kn_ref, vp_ref, vc_ref, vn_ref, g_ref, o_ref,
                   kx_ref, vx_ref, *, length, lq, radius):
    sub = ATTN_SUB
    _fill_window(kx_ref, kp_ref, kc_ref, kn_ref, radius)
    _fill_window(vx_ref, vp_ref, vc_ref, vn_ref, radius)
    for j in range(lq // sub):
        valid = _band_mask(pl.program_id(1) * lq + j * sub, sub, radius, length)
        q_rows = slice(j * sub, (j + 1) * sub)
        k_rows = slice(j * sub, (j + 1) * sub + 2 * radius)
        outs = []
        for h in range(B_HEADS):
            kv = slice((h // B_GROUP) * HEAD_DIM, (h // B_GROUP + 1) * HEAD_DIM)
            s = lax.dot_general(q_ref[q_rows, h * HEAD_DIM:(h + 1) * HEAD_DIM], kx_ref[k_rows, kv], _NT,
                                preferred_element_type=F32)
            s = jnp.where(valid, s, NEG_INF)
            sink = sink_ref[h]
            m = jnp.maximum(jnp.max(s, axis=-1, keepdims=True), sink)
            p = jnp.exp(s - m)
            l = jnp.sum(p, axis=-1, keepdims=True) + jnp.exp(sink - m)
            outs.append(jnp.dot(p.astype(BF16), vx_ref[k_rows, kv], preferred_element_type=F32) / l)
        o_ref[q_rows, :] = _rms(jnp.concatenate(outs, axis=-1), g_ref[...]).astype(o_ref.dtype)


def _attn_b(qb, kb, vb, sink, g_out_b, batch, seq):
    radius = B_RADIUS
    lq = min(ATTN_BLOCK, seq)
    per = lq // radius
    n_halo = seq // radius
    q3 = qb.reshape(batch, seq, B_WIDTH)
    k3 = kb.reshape(batch, seq, B_KV_WIDTH)
    v3 = vb.reshape(batch, seq, B_KV_WIDTH)
    qspec = pl.BlockSpec((None, lq, B_WIDTH), lambda b, i: (b, i, 0))
    cur = pl.BlockSpec((None, lq, B_KV_WIDTH), lambda b, i: (b, i, 0))
    prv = pl.BlockSpec((None, radius, B_KV_WIDTH), lambda b, i: (b, jnp.maximum(i * per - 1, 0), 0))
    nxt = pl.BlockSpec((None, radius, B_KV_WIDTH),
                       lambda b, i: (b, jnp.minimum((i + 1) * per, n_halo - 1), 0))
    ext = pltpu.VMEM((lq + 2 * radius, B_KV_WIDTH), BF16)
    out = pl.pallas_call(
        functools.partial(_attn_b_kernel, length=seq, lq=lq, radius=radius),
        grid=(batch, seq // lq),
        in_specs=[pl.BlockSpec(memory_space=pltpu.SMEM), qspec, prv, cur, nxt, prv, cur, nxt,
                  _resident((1, B_WIDTH))],
        out_specs=qspec,
        out_shape=jax.ShapeDtypeStruct((batch, seq, B_WIDTH), BF16),
        scratch_shapes=[ext, ext],
        compiler_params=_params(("parallel", "parallel")),
        name="attn_b",
    )(sink, q3, k3, k3, k3, v3, v3, v3, g_out_b)
    return out.reshape(batch * seq, B_WIDTH)


R_E1, R_E2, R_W1, R_W2, R_RANK1, R_RANK2 = range(6)


def _oproj_kernel(o1_ref, o4_ref, o16_ref, l1_ref, l4_ref, l16_ref, mb_ref, x_ref, wo_ref,
                  ga_ref, gf_ref, wr_ref, br_ref, cnt_in_ref, h_ref, u_ref, route_ref, route_t_ref, cnt_ref,
                  o4n_ref, o16n_ref, l4n_ref, l16n_ref, *, sub):
    tm = x_ref.shape[0]

    @pl.when(pl.program_id(0) == 0)
    def _():
        cnt_ref[...] = cnt_in_ref[...]

    for d, o_src, o_dst, l_src, l_dst in ((STRIDED_DILATIONS[0], o4_ref, o4n_ref, l4_ref, l4n_ref),
                                          (STRIDED_DILATIONS[1], o16_ref, o16n_ref, l16_ref, l16n_ref)):
        for r in range(d):
            rows = pl.ds(r, tm // d, stride=d)
            for h in range(A_HEADS):
                lo = r * A_WIDTH + h * HEAD_DIM
                o_dst[h, rows, :] = o_src[:, lo:lo + HEAD_DIM].astype(F32)
            l_dst[rows, :] = l_src[:, r * LANES:(r + 1) * LANES]

    for c in range(tm // sub):
        _oproj_rows(slice(c * sub, (c + 1) * sub), o1_ref, o4n_ref, o16n_ref, l1_ref, l4n_ref, l16n_ref,
                    mb_ref, x_ref, wo_ref, ga_ref, gf_ref, wr_ref, br_ref, h_ref, u_ref, route_ref,
                    route_t_ref, cnt_ref)


def _oproj_rows(rs, o1_ref, o2_ref, o3_ref, l1_ref, l2_ref, l3_ref, mb_ref, x_ref, wo_ref,
                ga_ref, gf_ref, wr_ref, br_ref, h_ref, u_ref, route_ref, route_t_ref, cnt_ref):
    tm = rs.stop - rs.start
    l1, l2, l3 = l1_ref[rs, :], l2_ref[rs, :], l3_ref[rs, :]
    mx = jnp.maximum(jnp.maximum(l1, l2), l3)
    e1, e2, e3 = jnp.exp(l1 - mx), jnp.exp(l2 - mx), jnp.exp(l3 - mx)
    inv = 1.0 / (e1 + e2 + e3)
    w1, w2, w3 = e1 * inv, e2 * inv, e3 * inv
    parts = []
    for h in range(A_HEADS):
        sl = slice(h * HEAD_DIM, (h + 1) * HEAD_DIM)
        bc = lambda w: jnp.broadcast_to(w[:, h:h + 1], (tm, HEAD_DIM))
        parts.append(bc(w1) * o1_ref[rs, sl].astype(F32) + bc(w2) * o2_ref[h, rs, :]
                     + bc(w3) * o3_ref[h, rs, :])
    mixed_a = _rms(jnp.concatenate(parts, axis=-1), ga_ref[...]).astype(BF16)
    h = (x_ref[rs, :]
         + jnp.dot(mixed_a, wo_ref[:A_WIDTH, :], preferred_element_type=F32)
         + jnp.dot(mb_ref[rs, :], wo_ref[A_WIDTH:, :], preferred_element_type=F32))
    h_ref[rs, :] = h
    u = _rms(h, gf_ref[...])
    u_ref[rs, :] = u

    u_hi = u.astype(BF16)
    u_lo = (u - u_hi.astype(F32)).astype(BF16)
    z = (jnp.dot(u_hi, wr_ref[...], preferred_element_type=F32)
         + jnp.dot(u_lo, wr_ref[...], preferred_element_type=F32))
    logits = z[:, :LANES] + z[:, LANES:] + br_ref[...]
    lane = lax.broadcasted_iota(jnp.int32, (tm, LANES), 1)
    big = jnp.int32(LANES)
    is_group = (lane >= N_EXPERTS) & (lane < N_EXPERTS + N_GROUPS)
    gmax = jnp.max(jnp.where(is_group, logits, -jnp.inf), axis=-1, keepdims=True)
    gidx = jnp.min(jnp.where(is_group & (logits == gmax), lane - N_EXPERTS, big), axis=-1, keepdims=True)
    ggate = 1.0 / jnp.sum(jnp.where(is_group, jnp.exp(logits - gmax), 0.0), axis=-1, keepdims=True)
    in_group = (lane >= gidx * EXPERTS_PER_GROUP) & (lane < (gidx + 1) * EXPERTS_PER_GROUP)
    v1 = jnp.max(jnp.where(in_group, logits, -jnp.inf), axis=-1, keepdims=True)
    i1 = jnp.min(jnp.where(in_group & (logits == v1), lane, big), axis=-1, keepdims=True)
    rest = in_group & (lane != i1)
    v2 = jnp.max(jnp.where(rest, logits, -jnp.inf), axis=-1, keepdims=True)
    i2 = jnp.min(jnp.where(rest & (logits == v2), lane, big), axis=-1, keepdims=True)
    t = jnp.exp(v2 - v1)
    tw1 = ggate / (1.0 + t)
    tw2 = ggate * t / (1.0 + t)

    onehot = ((lane == i1) | (lane == i2)).astype(F32)
    r_i = lax.broadcasted_iota(jnp.int32, (tm, tm), 0)
    c_i = lax.broadcasted_iota(jnp.int32, (tm, tm), 1)
    before = (c_i < r_i).astype(BF16)
    prefix = jnp.dot(before, onehot.astype(BF16), preferred_element_type=F32) + cnt_ref[...]
    rank1 = jnp.sum(jnp.where(lane == i1, prefix, 0.0), axis=-1, keepdims=True)
    rank2 = jnp.sum(jnp.where(lane == i2, prefix, 0.0), axis=-1, keepdims=True)
    cnt_ref[...] += jnp.sum(onehot, axis=0, keepdims=True)

    rec = jnp.zeros((tm, LANES), F32)
    for slot, val in ((R_E1, i1.astype(F32)), (R_E2, i2.astype(F32)), (R_W1, tw1), (R_W2, tw2),
                      (R_RANK1, rank1), (R_RANK2, rank2)):
        rec = jnp.where(lane == slot, val, rec)
    route_ref[rs, :] = rec
    route_t_ref[:, rs] = rec.T[:SUBLANES, :]


def _oproj(o_list, lse_list, mixed_b, x2d, w_o, g_out_a, g_ffn, w_route, b_route, cnt_in, tm, sub):
    T, D = x2d.shape
    row = lambda i: (i, 0)
    dils = (1,) + STRIDED_DILATIONS
    aspecs = [pl.BlockSpec((tm // d, d * A_WIDTH), row) for d in dils]
    lspecs = [pl.BlockSpec((tm // d, d * LANES), row) for d in dils]
    lspec = pl.BlockSpec((tm, LANES), row)
    dspec = pl.BlockSpec((tm, D), row)
    cspec = pl.BlockSpec((1, LANES), lambda i: (0, 0))
    return pl.pallas_call(
        functools.partial(_oproj_kernel, sub=sub),
        grid=(T // tm,),
        in_specs=[*aspecs, *lspecs, pl.BlockSpec((tm, B_WIDTH), row), dspec,
                  _resident(w_o.shape), _resident((1, A_WIDTH)), _resident((1, D)),
                  _resident(w_route.shape), _resident((1, LANES)), cspec],
        out_specs=[dspec, dspec, lspec, pl.BlockSpec((SUBLANES, tm), lambda i: (0, i)), cspec],
        out_shape=[jax.ShapeDtypeStruct((T, D), F32), jax.ShapeDtypeStruct((T, D), F32),
                   jax.ShapeDtypeStruct((T, LANES), F32), jax.ShapeDtypeStruct((SUBLANES, T), F32),
                   jax.ShapeDtypeStruct((1, LANES), F32)],
        scratch_shapes=[pltpu.VMEM((A_HEADS, tm, HEAD_DIM), F32), pltpu.VMEM((A_HEADS, tm, HEAD_DIM), F32),
                        pltpu.VMEM((tm, LANES), F32), pltpu.VMEM((tm, LANES), F32)],
        compiler_params=_params(("arbitrary",)),
        name="oproj_route",
    )(*o_list, *lse_list, mixed_b, x2d, w_o, g_out_a, g_ffn, w_route, b_route, cnt_in)


def _row_copy(src_ref, src_row, dst_ref, dst_row, sem):
    return pltpu.make_async_copy(src_ref.at[pl.ds(src_row, 1)], dst_ref.at[pl.ds(dst_row, 1)], sem)


def _sorted_rows(off_ref, rec_ref, j):
    return (off_ref[rec_ref[R_E1, j]] + rec_ref[R_RANK1, j], off_ref[rec_ref[R_E2, j]] + rec_ref[R_RANK2, j])


def _dispatch_kernel(off_ref, fill_ref, *rest, tm, tiles_per_src, fill_rows):
    n_src = len(tiles_per_src)
    rec_refs = rest[:n_src]
    u_refs = rest[n_src:2 * n_src]
    xs_ref, sem, zero_ref, fill_sem = rest[2 * n_src:]
    i = pl.program_id(0)

    @pl.when(i == 0)
    def _():
        zero_ref[...] = jnp.zeros_like(zero_ref)

        def tile_copy(k):
            start = pl.multiple_of(fill_ref[k], fill_rows)
            return pltpu.make_async_copy(zero_ref, xs_ref.at[pl.ds(start, fill_rows)], fill_sem)

        for k in range(fill_ref.shape[0]):
            @pl.when(fill_ref[k] >= 0)
            def _():
                tile_copy(k).start()

        for k in range(fill_ref.shape[0]):
            @pl.when(fill_ref[k] >= 0)
            def _():
                tile_copy(k).wait()

    def scatter(rec_ref, u_ref):
        def issue(j, carry):
            for row in _sorted_rows(off_ref, rec_ref, j):
                _row_copy(u_ref, j, xs_ref, row, sem).start()
            return carry

        lax.fori_loop(0, tm, issue, 0, unroll=8)

        def drain(j, carry):
            _row_copy(u_ref, 0, xs_ref, 0, sem).wait()
            return carry

        lax.fori_loop(0, 2 * tm, drain, 0, unroll=8)

    first = 0
    for rec_ref, u_ref, n in zip(rec_refs, u_refs, tiles_per_src):
        @pl.when((i >= first) & (i < first + n))
        def _():
            scatter(rec_ref, u_ref)

        first += n


def _dispatch(us, recs, offsets, fill_starts, n_rows, tm, fill_rows):
    D = us[0].shape[1]
    tiles_per_src = tuple(u.shape[0] // tm for u in us)
    in_specs = [pl.BlockSpec(memory_space=pltpu.SMEM), pl.BlockSpec(memory_space=pltpu.SMEM)]
    starts = [sum(tiles_per_src[:k]) for k in range(len(us))]
    for first, n in zip(starts, tiles_per_src):
        in_specs.append(pl.BlockSpec((SUBLANES, tm), lambda i, first=first, n=n: (0, jnp.clip(i - first, 0, n - 1)),
                                     memory_space=pltpu.SMEM))
    for first, n in zip(starts, tiles_per_src):
        in_specs.append(pl.BlockSpec((tm, D), lambda i, first=first, n=n: (jnp.clip(i - first, 0, n - 1), 0)))
    first = sum(tiles_per_src)
    return pl.pallas_call(
        functools.partial(_dispatch_kernel, tm=tm, tiles_per_src=tiles_per_src, fill_rows=fill_rows),
        grid=(first,),
        in_specs=in_specs,
        out_specs=pl.BlockSpec(memory_space=pl.ANY),
        out_shape=jax.ShapeDtypeStruct((n_rows, D), F32),
        scratch_shapes=[pltpu.SemaphoreType.DMA(()), pltpu.VMEM((fill_rows, D), F32),
                        pltpu.SemaphoreType.DMA(())],
        compiler_params=_params(("arbitrary",)),
        name="moe_dispatch",
    )(offsets, fill_starts, *recs, *us)


def _experts_kernel(te_ref, used_ref, x_ref, w1_ref, w3_ref, w2_ref, y_ref, w1b_ref, w3b_ref, w2b_ref):
    i = pl.program_id(0)
    cur = jnp.minimum(i, used_ref[0] - 1)
    new_expert = (i == 0) | (te_ref[cur] != te_ref[jnp.maximum(cur - 1, 0)])

    @pl.when(new_expert & (i < used_ref[0]))
    def _():
        w1b_ref[...] = w1_ref[...].astype(BF16)
        w3b_ref[...] = w3_ref[...].astype(BF16)
        w2b_ref[...] = w2_ref[...].astype(BF16)

    @pl.when(i < used_ref[0])
    def _():
        x = x_ref[...].astype(BF16)
        a = jnp.dot(x, w1b_ref[...], preferred_element_type=F32)
        b = jnp.dot(x, w3b_ref[...], preferred_element_type=F32)
        hid = (a * jax.nn.sigmoid(a)) * b
        y_ref[...] = jnp.dot(hid.astype(BF16), w2b_ref[...], preferred_element_type=F32)

    @pl.when(i >= used_ref[0])
    def _():
        y_ref[...] = jnp.zeros_like(y_ref)


def _experts(xs, tile_expert, n_used, w1, w3, w2, tm):
    P, D = xs.shape
    F = w1.shape[2]
    row = lambda i, te, used: (jnp.minimum(i, used[0] - 1), 0)
    wmap = lambda i, te, used: (te[jnp.minimum(i, used[0] - 1)], 0, 0)
    grid_spec = pltpu.PrefetchScalarGridSpec(
        num_scalar_prefetch=2,
        grid=(P // tm,),
        in_specs=[pl.BlockSpec((tm, D), row),
                  pl.BlockSpec((None, D, F), wmap),
                  pl.BlockSpec((None, D, F), wmap),
                  pl.BlockSpec((None, F, D), wmap)],
        out_specs=pl.BlockSpec((tm, D), lambda i, te, used: (i, 0)),
        scratch_shapes=[pltpu.VMEM((D, F), BF16), pltpu.VMEM((D, F), BF16), pltpu.VMEM((F, D), BF16)],
    )
    return pl.pallas_call(
        _experts_kernel,
        grid_spec=grid_spec,
        out_shape=jax.ShapeDtypeStruct((P, D), F32),
        compiler_params=_params(("arbitrary",)),
        name="moe_experts",
    )(tile_expert, n_used, xs, w1, w3, w2)


def _ple_kernel(off_ref, rec_ref, rec_next_ref, ys_ref, h_ref, route_ref, p_ref, gp_ref, wg_ref, bg_ref,
                wp_ref, gf_ref, o_ref, ybuf_a, ybuf_b, sems, *, tm):
    i = pl.program_id(0)
    half = tm // 2
    bufs = (ybuf_a, ybuf_b)

    def row_copies(rec_ref_, s, j, base):
        row1, row2 = _sorted_rows(off_ref, rec_ref_, j)
        return (_row_copy(ys_ref, row1, bufs[s].at[0], j - base, sems.at[s]),
                _row_copy(ys_ref, row2, bufs[s].at[1], j - base, sems.at[s]))

    def drain(s):
        def wait(j, carry):
            _row_copy(ys_ref, 0, bufs[s].at[0], 0, sems.at[s]).wait()
            return carry

        lax.fori_loop(0, 2 * half, wait, 0, unroll=8)

    n_chunks = 4
    width = h_ref.shape[1] // n_chunks

    def compute(s, rows, requests):
        route = route_ref[rows, :]
        h = h_ref[rows, :] + route[:, R_W1:R_W1 + 1] * bufs[s][0] + route[:, R_W2:R_W2 + 1] * bufs[s][1]
        u = _rms(h, gp_ref[...]).astype(BF16)
        p = p_ref[rows, :].astype(BF16)
        per = len(requests) // n_chunks
        parts = []
        for c in range(n_chunks):
            for request in requests[c * per:(c + 1) * per]:
                for copy in row_copies(*request):
                    copy.start()
            cols = slice(c * width, (c + 1) * width)
            gate = jax.nn.sigmoid(jnp.dot(u, wg_ref[:, cols], preferred_element_type=F32) + bg_ref[:, cols])
            proj = jnp.dot(p, wp_ref[:, cols], preferred_element_type=F32)
            parts.append(h[:, cols] + gate * proj)
        o_ref[rows, :] = _rms(jnp.concatenate(parts, axis=-1), gf_ref[...])

    @pl.when(i == 0)
    def _():
        def issue(j, carry):
            for copy in row_copies(rec_ref, 0, j, 0):
                copy.start()
            return carry

        lax.fori_loop(0, half, issue, 0, unroll=8)

    drain(0)
    compute(0, slice(0, half), [(rec_ref, 1, j, half) for j in range(half, tm)])
    drain(1)
    compute(1, slice(half, tm), [(rec_next_ref, 0, j, 0) for j in range(half)])

    @pl.when(i + 1 == pl.num_programs(0))
    def _():
        drain(0)


def _ple(rec, offsets, ys, h, route, p2d, g_ple, w_gate, b_gate, w_proj, g_final, tm):
    T, D = h.shape
    P = p2d.shape[1]
    row = lambda i: (i, 0)
    n = T // tm
    return pl.pallas_call(
        functools.partial(_ple_kernel, tm=tm),
        grid=(n,),
        in_specs=[pl.BlockSpec(memory_space=pltpu.SMEM),
                  pl.BlockSpec((SUBLANES, tm), lambda i: (0, i), memory_space=pltpu.SMEM),
                  pl.BlockSpec((SUBLANES, tm), lambda i: (0, jnp.minimum(i + 1, n - 1)),
                               memory_space=pltpu.SMEM),
                  pl.BlockSpec(memory_space=pl.ANY),
                  pl.BlockSpec((tm, D), row), pl.BlockSpec((tm, LANES), row), pl.BlockSpec((tm, P), row),
                  _resident((1, D)), _resident(w_gate.shape), _resident((1, D)), _resident(w_proj.shape),
                  _resident((1, D))],
        out_specs=pl.BlockSpec((tm, D), row),
        out_shape=jax.ShapeDtypeStruct((T, D), F32),
        scratch_shapes=[pltpu.VMEM((2, tm // 2, D), F32), pltpu.VMEM((2, tm // 2, D), F32),
                        pltpu.SemaphoreType.DMA((2,))],
        compiler_params=_params(("arbitrary",)),
        name="ple_final",
    )(offsets, rec, rec, ys, h, route, p2d, g_ple, w_gate, b_gate, w_proj, g_final)


def _rope_tables(seq):
    half = ROPE_DIM // 2
    inv = ROPE_THETA ** (-(jnp.arange(half, dtype=F32) * 2.0) / ROPE_DIM)
    ang = jnp.arange(seq, dtype=F32)[:, None] * inv[None, :]
    cos, sin = jnp.cos(ang), jnp.sin(ang)
    zeros = jnp.zeros((seq, LANES - ROPE_DIM), F32)
    z16 = jnp.zeros((seq, half), F32)
    cos_t = jnp.concatenate([cos, cos, jnp.ones((seq, LANES - ROPE_DIM), F32)], axis=-1)
    sa_t = jnp.concatenate([z16, sin, zeros], axis=-1)
    sb_t = jnp.concatenate([-sin, z16, zeros], axis=-1)
    return cos_t, sa_t, sb_t


MOE_TILE = 256
OPROJ_TILE = 512
PLE_TILE = 512
DISPATCH_TILE = 1024


def _mixer(x, wts, cnt_in):
    batch, seq, D = x.shape
    T = batch * seq
    x2d = x.reshape(T, D)
    cos, sa, sb = _rope_tables(seq)
    qa, ka, va, qb, kb, vb, *strided = _qkv(x2d, wts["g_attn"], wts["w_in"], cos, sa, sb, seq, tm=256)
    n_dil = len(STRIDED_DILATIONS)
    qkv_by_dil = {1: (qa, ka, va)}
    for n, d in enumerate(STRIDED_DILATIONS):
        qkv_by_dil[d] = tuple(strided[i * n_dil + n] for i in range(3))
    o_list, lse_list = [], []
    for window, dil in DILATED_CONFIGS:
        o, lse = _attn_a(*qkv_by_dil[dil], batch, seq, dil, window // (2 * dil))
        o_list.append(o)
        lse_list.append(lse)
    mixed_b = _attn_b(qb, kb, vb, wts["sink"], wts["g_out_b"], batch, seq)
    return _oproj(o_list, lse_list, mixed_b, x2d, wts["w_o"], wts["g_out_a"], wts["g_ffn"],
                  wts["w_route"], wts["b_route"], cnt_in, tm=OPROJ_TILE, sub=OPROJ_TILE // 2)


def _sorted_layout(counts, n_tiles, tm):
    cnt = counts[0, :N_EXPERTS].astype(jnp.int32)
    tiles = (cnt + tm - 1) // tm
    ends = jnp.cumsum(tiles)
    offsets = (ends - tiles) * tm
    tile_expert = jnp.sum(jnp.arange(n_tiles, dtype=jnp.int32)[:, None] >= ends[None, :], axis=1)
    tile_expert = jnp.minimum(tile_expert, N_EXPERTS - 1).astype(jnp.int32)
    n_used = ends[-1]
    last_tile_start = jnp.where(tiles > 0, (ends - 1) * tm, -1)
    tail = n_used + jnp.arange(N_EXPERTS, dtype=jnp.int32)
    tail_start = jnp.where(tail < n_tiles, tail * tm, -1)
    fill_starts = jnp.concatenate([last_tile_start, tail_start]).astype(jnp.int32)
    return offsets, tile_expert, n_used.reshape(1).astype(jnp.int32), fill_starts


def kernel(x_prompt, x_sample, p_prompt, p_sample, g_attn, w_in, g_out_a, g_out_b, sink, w_o, g_ffn, w_group, b_group, w_router, b_router, w1, w3, w2, g_ple, w_ple_gate, b_ple_gate, w_ple_proj, g_final):
    depth = w_in.shape[0]
    assert depth == 1
    l = 0
    D = w_in.shape[1]
    pad = LANES - N_EXPERTS - N_GROUPS
    w_route = jnp.pad(jnp.concatenate([w_router[l], w_group[l]], axis=-1), ((0, 0), (0, pad)))
    w_route_hi = w_route.astype(BF16)
    w_route_lo = (w_route - w_route_hi.astype(F32)).astype(BF16)
    wts = {
        "g_attn": g_attn[l].reshape(1, D),
        "w_in": w_in[l].astype(BF16),
        "g_out_a": g_out_a[l].reshape(1, A_WIDTH),
        "g_out_b": g_out_b[l].reshape(1, B_WIDTH),
        "sink": sink[l],
        "w_o": w_o[l].astype(BF16),
        "g_ffn": g_ffn[l].reshape(1, D),
        "w_route": jnp.concatenate([w_route_hi, w_route_lo], axis=-1),
        "b_route": jnp.pad(jnp.concatenate([b_router[l], b_group[l]]), (0, pad)).reshape(1, LANES),
        "w1": w1[l],
        "w3": w3[l],
        "w2": w2[l],
        "g_ple": g_ple[l].reshape(1, D),
        "w_ple_gate": w_ple_gate[l].astype(BF16),
        "b_ple_gate": b_ple_gate[l].reshape(1, D),
        "w_ple_proj": w_ple_proj[l].astype(BF16),
        "g_final": g_final.reshape(1, D),
    }
    xs_list = (x_prompt, x_sample)
    ps_list = (p_prompt[l], p_sample[l])

    counts = jnp.zeros((1, LANES), F32)
    mixed = []
    route_ts = []
    for x in xs_list:
        h1, u2, route, route_t, counts = _mixer(x, wts, counts)
        mixed.append((h1, u2, route))
        route_ts.append(route_t)

    n_pairs = 2 * sum(x.shape[0] * x.shape[1] for x in xs_list)
    n_tiles = n_pairs // MOE_TILE + N_EXPERTS
    offsets, tile_expert, n_used, fill_starts = _sorted_layout(counts, n_tiles, MOE_TILE)
    recs = [route_t.astype(jnp.int32) for route_t in route_ts]
    xs = _dispatch([u2 for _, u2, _ in mixed], recs, offsets, fill_starts, n_tiles * MOE_TILE,
                   DISPATCH_TILE, MOE_TILE)
    ys = _experts(xs, tile_expert, n_used, wts["w1"], wts["w3"], wts["w2"], MOE_TILE)

    outs = []
    for x, p, (h1, u2, route), rec in zip(xs_list, ps_list, mixed, recs):
        y = _ple(rec, offsets, ys, h1, route, p.reshape(h1.shape[0], -1),
                 wts["g_ple"], wts["w_ple_gate"], wts["b_ple_gate"], wts["w_ple_proj"], wts["g_final"],
                 PLE_TILE)
        outs.append(y.reshape(x.shape))
    return tuple(outs)
```

```python
import functools

import jax
import jax.numpy as jnp
import numpy as np
from jax import lax
from jax.experimental import pallas as pl
from jax.experimental.pallas import tpu as pltpu

HEAD_DIM = 128
A_HEADS = 8
A_WIDTH = A_HEADS * HEAD_DIM
B_HEADS = 8
B_KV_HEADS = 2
B_GROUP = B_HEADS // B_KV_HEADS
B_WIDTH = B_HEADS * HEAD_DIM
B_KV_WIDTH = B_KV_HEADS * HEAD_DIM
DILATED_CONFIGS = ((128, 1), (512, 4), (2048, 16))
B_RADIUS = 128
ROPE_THETA = 500000.0
ROPE_DIM = HEAD_DIM // 4
N_GROUPS = 4
EXPERTS_PER_GROUP = 8
N_EXPERTS = N_GROUPS * EXPERTS_PER_GROUP
RMS_EPS = 1e-6
NEG_INF = -1e30

LANES = 128
SUBLANES = 8
VMEM_LIMIT = 56 * 1024 * 1024

F32 = jnp.float32
BF16 = jnp.bfloat16


def _resident(shape):
    nd = len(shape)
    return pl.BlockSpec(shape, lambda *_: (0,) * nd, pipeline_mode=pl.Buffered(1))


def _params(sem):
    return pltpu.CompilerParams(dimension_semantics=sem, vmem_limit_bytes=VMEM_LIMIT)


def _rms(x, g):
    return x * lax.rsqrt(jnp.mean(x * x, axis=-1, keepdims=True) + RMS_EPS) * g


STRIDED_DILATIONS = tuple(d for _, d in DILATED_CONFIGS if d > 1)


def _qkv_kernel(x_ref, g_ref, w_ref, cos_ref, sa_ref, sb_ref, *rest):
    n_dil = len(STRIDED_DILATIONS)
    perm_refs = rest[:n_dil]
    qa_ref, ka_ref, va_ref, qb_ref, kb_ref, vb_ref = rest[n_dil:n_dil + 6]
    dil_refs = [rest[n_dil + 6 + i * n_dil:n_dil + 6 + (i + 1) * n_dil] for i in range(3)]
    tm = x_ref.shape[0]
    u = _rms(x_ref[...], g_ref[...]).astype(BF16)
    cos = cos_ref[...]
    sa = sa_ref[...]
    sb = sb_ref[...]
    half = ROPE_DIM // 2
    scale = HEAD_DIM ** -0.5

    def rope(t):
        return t * cos + pltpu.roll(t, half, 1) * sa + pltpu.roll(t, LANES - half, 1) * sb

    sections = (
        (qa_ref, 0, A_HEADS, True, scale, 0),
        (ka_ref, A_WIDTH, A_HEADS, True, None, 1),
        (va_ref, 2 * A_WIDTH, A_HEADS, False, None, 2),
        (qb_ref, 3 * A_WIDTH, B_HEADS, True, scale, None),
        (kb_ref, 3 * A_WIDTH + B_WIDTH, B_KV_HEADS, True, None, None),
        (vb_ref, 3 * A_WIDTH + B_WIDTH + B_KV_WIDTH, B_KV_HEADS, False, None, None),
    )
    for out_ref, c0, heads, rotary, sc, a_idx in sections:
        for h0 in range(0, heads, 4):
            nh = min(4, heads - h0)
            lo = c0 + h0 * HEAD_DIM
            acc = jnp.dot(u, w_ref[:, lo:lo + nh * HEAD_DIM], preferred_element_type=F32)
            for j in range(nh):
                t = acc[:, j * HEAD_DIM:(j + 1) * HEAD_DIM]
                if rotary:
                    t = rope(t)
                if sc is not None:
                    t = t * sc
                cols = slice((h0 + j) * HEAD_DIM, (h0 + j + 1) * HEAD_DIM)
                out_ref[:, cols] = t.astype(BF16)
        if a_idx is not None:
            for d, perm_ref, dref in zip(STRIDED_DILATIONS, perm_refs, dil_refs[a_idx]):
                n = tm // d
                permuted = jnp.dot(perm_ref[...], out_ref[...], preferred_element_type=F32).astype(BF16)
                for r in range(d):
                    dref[:, r * A_WIDTH:(r + 1) * A_WIDTH] = permuted[r * n:(r + 1) * n, :]


def _qkv(x2d, g, w_in, cos, sa, sb, seq, tm):
    T, D = x2d.shape
    n_pos = seq // tm
    row = lambda i: (i, 0)
    tab = pl.BlockSpec((tm, LANES), lambda i: (i % n_pos, 0))
    widths = (A_WIDTH, A_WIDTH, A_WIDTH, B_WIDTH, B_KV_WIDTH, B_KV_WIDTH)
    out_specs = [pl.BlockSpec((tm, w), row) for w in widths]
    out_shape = [jax.ShapeDtypeStruct((T, w), BF16) for w in widths]
    for _ in range(3):
        for d in STRIDED_DILATIONS:
            out_specs.append(pl.BlockSpec((tm // d, d * A_WIDTH), row))
            out_shape.append(jax.ShapeDtypeStruct((T // d, d * A_WIDTH), BF16))
    perms = [_residue_permutation(tm, d) for d in STRIDED_DILATIONS]
    return pl.pallas_call(
        _qkv_kernel,
        grid=(T // tm,),
        in_specs=[pl.BlockSpec((tm, D), row), _resident((1, D)), _resident(w_in.shape), tab, tab, tab,
                  *[_resident((tm, tm)) for _ in perms]],
        out_specs=out_specs,
        out_shape=out_shape,
        compiler_params=_params(("parallel",)),
        name="qkv",
    )(x2d, g, w_in, cos, sa, sb, *perms)


def _residue_permutation(n, d):
    dst = np.arange(n)
    src = (dst % (n // d)) * d + dst // (n // d)
    p = np.zeros((n, n), np.float32)
    p[dst, src] = 1.0
    return jnp.asarray(p, BF16)


ATTN_SUB = 128
ATTN_BLOCK = 1024


def _band_mask(first_q, sub, radius, length):
    shape = (sub, sub + 2 * radius)
    row = lax.broadcasted_iota(jnp.int32, shape, 0)
    col = lax.broadcasted_iota(jnp.int32, shape, 1)
    kpos = first_q - radius + col
    return (jnp.abs(col - radius - row) <= radius) & (kpos >= 0) & (kpos < length)


def _fill_window(ext_ref, prev_ref, cur_ref, next_ref, radius):
    n = cur_ref.shape[0]
    ext_ref[0:radius, :] = prev_ref[...]
    ext_ref[radius:radius + n, :] = cur_ref[...]
    ext_ref[radius + n:, :] = next_ref[...]


_NT = (((1,), (1,)), ((), ()))


def _attn_a_kernel(q_ref, kp_ref, kc_ref, kn_ref, vp_ref, vc_ref, vn_ref, o_ref, lse_ref, kx_ref, vx_ref,
                   *, length, lq, radius, n_res):
    sub = min(ATTN_SUB, lq)
    _fill_window(kx_ref, kp_ref, kc_ref, kn_ref, radius)
    _fill_window(vx_ref, vp_ref, vc_ref, vn_ref, radius)
    lane = lax.broadcasted_iota(jnp.int32, (sub, LANES), 1)
    for j in range(lq // sub):
        valid = _band_mask(pl.program_id(2) * lq + j * sub, sub, radius, length)
        q_rows = slice(j * sub, (j + 1) * sub)
        k_rows = slice(j * sub, (j + 1) * sub + 2 * radius)
        for res in range(n_res):
            cols = [slice(res * A_WIDTH + h * HEAD_DIM, res * A_WIDTH + (h + 1) * HEAD_DIM)
                    for h in range(A_HEADS)]
            q = jnp.stack([q_ref[q_rows, sl] for sl in cols], axis=0)
            k = jnp.stack([kx_ref[k_rows, sl] for sl in cols], axis=0)
            v = jnp.stack([vx_ref[k_rows, sl] for sl in cols], axis=0)
            s = jnp.einsum("hqd,hkd->hqk", q, k, preferred_element_type=F32)
            s = jnp.where(valid[None], s, NEG_INF)
            m = jnp.max(s, axis=-1, keepdims=True)
            p = jnp.exp(s - m)
            l = jnp.sum(p, axis=-1, keepdims=True)
            o = jnp.einsum("hqk,hkd->hqd", p.astype(BF16), v, preferred_element_type=F32) / l
            lse = m + jnp.log(l)
            lse_all = jnp.zeros((sub, LANES), F32)
            for h, sl in enumerate(cols):
                o_ref[q_rows, sl] = o[h].astype(o_ref.dtype)
                lse_all = jnp.where(lane == h, lse[h], lse_all)
            lse_ref[q_rows, res * LANES:(res + 1) * LANES] = lse_all


def _attn_a(qa, ka, va, batch, seq, dil, radius):
    ls = seq // dil
    lq = min(ATTN_BLOCK, ls)
    n_res = min(dil, ATTN_BLOCK // lq)
    per = lq // radius
    n_halo = ls // radius
    view = lambda t: t.reshape(batch, ls, dil * A_WIDTH)
    w = n_res * A_WIDTH
    cur = pl.BlockSpec((None, lq, w), lambda b, r, i: (b, i, r))
    prv = pl.BlockSpec((None, radius, w), lambda b, r, i: (b, jnp.maximum(i * per - 1, 0), r))
    nxt = pl.BlockSpec((None, radius, w), lambda b, r, i: (b, jnp.minimum((i + 1) * per, n_halo - 1), r))
    o, lse = pl.pallas_call(
        functools.partial(_attn_a_kernel, length=ls, lq=lq, radius=radius, n_res=n_res),
        grid=(batch, dil // n_res, ls // lq),
        in_specs=[cur, prv, cur, nxt, prv, cur, nxt],
        out_specs=[cur, pl.BlockSpec((None, lq, n_res * LANES), lambda b, r, i: (b, i, r))],
        out_shape=[jax.ShapeDtypeStruct((batch, ls, dil * A_WIDTH), BF16),
                   jax.ShapeDtypeStruct((batch, ls, dil * LANES), F32)],
        scratch_shapes=[pltpu.VMEM((lq + 2 * radius, w), BF16), pltpu.VMEM((lq + 2 * radius, w), BF16)],
        compiler_params=_params(("parallel", "parallel", "parallel")),
        name=f"attn_a_d{dil}",
    )(view(qa), view(ka), view(ka), view(ka), view(va), view(va), view(va))
    return o.reshape(batch * ls, dil * A_WIDTH), lse.reshape(batch * ls, dil * LANES)


def _attn_b_kernel(sink_ref, q_ref, kp_ref, kc_ref, kn_ref, vp_ref, vc_ref, vn_ref, g_ref, o_ref,
                   kx_ref, vx_ref, *, length, lq, radius):
    sub = ATTN_SUB
    _fill_window(kx_ref, kp_ref, kc_ref, kn_ref, radius)
    _fill_window(vx_ref, vp_ref, vc_ref, vn_ref, radius)
    for j in range(lq // sub):
        valid = _band_mask(pl.program_id(1) * lq + j * sub, sub, radius, length)
        q_rows = slice(j * sub, (j + 1) * sub)
        k_rows = slice(j * sub, (j + 1) * sub + 2 * radius)
        outs = []
        for g in range(B_KV_HEADS):
            kv = slice(g * HEAD_DIM, (g + 1) * HEAD_DIM)
            heads = range(g * B_GROUP, (g + 1) * B_GROUP)
            q = jnp.concatenate([q_ref[q_rows, h * HEAD_DIM:(h + 1) * HEAD_DIM] for h in heads], axis=0)
            s = lax.dot_general(q, kx_ref[k_rows, kv], _NT, preferred_element_type=F32)
            s = jnp.where(valid[None], s.reshape(B_GROUP, sub, sub + 2 * radius), NEG_INF)
            head = lax.broadcasted_iota(jnp.int32, (B_GROUP, 1, 1), 0)
            sink = jnp.zeros((B_GROUP, 1, 1), F32)
            for n, h in enumerate(heads):
                sink = jnp.where(head == n, sink_ref[h], sink)
            m = jnp.maximum(jnp.max(s, axis=-1, keepdims=True), sink)
            p = jnp.exp(s - m)
            l = jnp.sum(p, axis=-1, keepdims=True) + jnp.exp(sink - m)
            o = jnp.dot(p.reshape(B_GROUP * sub, sub + 2 * radius).astype(BF16), vx_ref[k_rows, kv],
                        preferred_element_type=F32).reshape(B_GROUP, sub, HEAD_DIM) / l
            outs.extend(o[n] for n in range(B_GROUP))
        o_ref[q_rows, :] = _rms(jnp.concatenate(outs, axis=-1), g_ref[...]).astype(o_ref.dtype)


def _attn_b(qb, kb, vb, sink, g_out_b, batch, seq):
    radius = B_RADIUS
    lq = min(ATTN_BLOCK, seq)
    per = lq // radius
    n_halo = seq // radius
    q3 = qb.reshape(batch, seq, B_WIDTH)
    k3 = kb.reshape(batch, seq, B_KV_WIDTH)
    v3 = vb.reshape(batch, seq, B_KV_WIDTH)
    qspec = pl.BlockSpec((None, lq, B_WIDTH), lambda b, i: (b, i, 0))
    cur = pl.BlockSpec((None, lq, B_KV_WIDTH), lambda b, i: (b, i, 0))
    prv = pl.BlockSpec((None, radius, B_KV_WIDTH), lambda b, i: (b, jnp.maximum(i * per - 1, 0), 0))
    nxt = pl.BlockSpec((None, radius, B_KV_WIDTH),
                       lambda b, i: (b, jnp.minimum((i + 1) * per, n_halo - 1), 0))
    ext = pltpu.VMEM((lq + 2 * radius, B_KV_WIDTH), BF16)
    out = pl.pallas_call(
        functools.partial(_attn_b_kernel, length=seq, lq=lq, radius=radius),
        grid=(batch, seq // lq),
        in_specs=[pl.BlockSpec(memory_space=pltpu.SMEM), qspec, prv, cur, nxt, prv, cur, nxt,
                  _resident((1, B_WIDTH))],
        out_specs=qspec,
        out_shape=jax.ShapeDtypeStruct((batch, seq, B_WIDTH), BF16),
        scratch_shapes=[ext, ext],
        compiler_params=_params(("parallel", "parallel")),
        name="attn_b",
    )(sink, q3, k3, k3, k3, v3, v3, v3, g_out_b)
    return out.reshape(batch * seq, B_WIDTH)


R_E1, R_E2, R_W1, R_W2, R_RANK1, R_RANK2 = range(6)


def _oproj_kernel(o1_ref, o4_ref, o16_ref, l1_ref, l4_ref, l16_ref, mb_ref, x_ref, wo_ref,
                  ga_ref, gf_ref, wr_ref, br_ref, cnt_in_ref, h_ref, u_ref, route_ref, route_t_ref, cnt_ref,
                  o4n_ref, o16n_ref, l4n_ref, l16n_ref, *, sub):
    tm = x_ref.shape[0]

    @pl.when(pl.program_id(0) == 0)
    def _():
        cnt_ref[...] = cnt_in_ref[...]

    for d, o_src, o_dst, l_src, l_dst in ((STRIDED_DILATIONS[0], o4_ref, o4n_ref, l4_ref, l4n_ref),
                                          (STRIDED_DILATIONS[1], o16_ref, o16n_ref, l16_ref, l16n_ref)):
        for r in range(d):
            rows = pl.ds(r, tm // d, stride=d)
            for h in range(A_HEADS):
                lo = r * A_WIDTH + h * HEAD_DIM
                o_dst[h, rows, :] = o_src[:, lo:lo + HEAD_DIM].astype(F32)
            l_dst[rows, :] = l_src[:, r * LANES:(r + 1) * LANES]

    for c in range(tm // sub):
        _oproj_rows(slice(c * sub, (c + 1) * sub), o1_ref, o4n_ref, o16n_ref, l1_ref, l4n_ref, l16n_ref,
                    mb_ref, x_ref, wo_ref, ga_ref, gf_ref, wr_ref, br_ref, h_ref, u_ref, route_ref,
                    route_t_ref, cnt_ref)


def _oproj_rows(rs, o1_ref, o2_ref, o3_ref, l1_ref, l2_ref, l3_ref, mb_ref, x_ref, wo_ref,
                ga_ref, gf_ref, wr_ref, br_ref, h_ref, u_ref, route_ref, route_t_ref, cnt_ref):
    tm = rs.stop - rs.start
    l1, l2, l3 = l1_ref[rs, :], l2_ref[rs, :], l3_ref[rs, :]
    mx = jnp.maximum(jnp.maximum(l1, l2), l3)
    e1, e2, e3 = jnp.exp(l1 - mx), jnp.exp(l2 - mx), jnp.exp(l3 - mx)
    inv = 1.0 / (e1 + e2 + e3)
    w1, w2, w3 = e1 * inv, e2 * inv, e3 * inv
    parts = []
    for h in range(A_HEADS):
        sl = slice(h * HEAD_DIM, (h + 1) * HEAD_DIM)
        bc = lambda w: jnp.broadcast_to(w[:, h:h + 1], (tm, HEAD_DIM))
        parts.append(bc(w1) * o1_ref[rs, sl].astype(F32) + bc(w2) * o2_ref[h, rs, :]
                     + bc(w3) * o3_ref[h, rs, :])
    mixed_a = _rms(jnp.concatenate(parts, axis=-1), ga_ref[...]).astype(BF16)
    h = (x_ref[rs, :]
         + jnp.dot(mixed_a, wo_ref[:A_WIDTH, :], preferred_element_type=F32)
         + jnp.dot(mb_ref[rs, :], wo_ref[A_WIDTH:, :], preferred_element_type=F32))
    h_ref[rs, :] = h
    u = _rms(h, gf_ref[...])
    u_ref[rs, :] = u

    u_hi = u.astype(BF16)
    u_lo = (u - u_hi.astype(F32)).astype(BF16)
    z = (jnp.dot(u_hi, wr_ref[...], preferred_element_type=F32)
         + jnp.dot(u_lo, wr_ref[...], preferred_element_type=F32))
    logits = z[:, :LANES] + z[:, LANES:] + br_ref[...]
    lane = lax.broadcasted_iota(jnp.int32, (tm, LANES), 1)
    big = jnp.int32(LANES)
    is_group = (lane >= N_EXPERTS) & (lane < N_EXPERTS + N_GROUPS)
    gmax = jnp.max(jnp.where(is_group, logits, -jnp.inf), axis=-1, keepdims=True)
    gidx = jnp.min(jnp.where(is_group & (logits == gmax), lane - N_EXPERTS, big), axis=-1, keepdims=True)
    ggate = 1.0 / jnp.sum(jnp.where(is_group, jnp.exp(logits - gmax), 0.0), axis=-1, keepdims=True)
    in_group = (lane >= gidx * EXPERTS_PER_GROUP) & (lane < (gidx + 1) * EXPERTS_PER_GROUP)
    v1 = jnp.max(jnp.where(in_group, logits, -jnp.inf), axis=-1, keepdims=True)
    i1 = jnp.min(jnp.where(in_group & (logits == v1), lane, big), axis=-1, keepdims=True)
    rest = in_group & (lane != i1)
    v2 = jnp.max(jnp.where(rest, logits, -jnp.inf), axis=-1, keepdims=True)
    i2 = jnp.min(jnp.where(rest & (logits == v2), lane, big), axis=-1, keepdims=True)
    t = jnp.exp(v2 - v1)
    tw1 = ggate / (1.0 + t)
    tw2 = ggate * t / (1.0 + t)

    onehot = ((lane == i1) | (lane == i2)).astype(F32)
    r_i = lax.broadcasted_iota(jnp.int32, (tm, tm), 0)
    c_i = lax.broadcasted_iota(jnp.int32, (tm, tm), 1)
    before = (c_i < r_i).astype(BF16)
    prefix = jnp.dot(before, onehot.astype(BF16), preferred_element_type=F32) + cnt_ref[...]
    rank1 = jnp.sum(jnp.where(lane == i1, prefix, 0.0), axis=-1, keepdims=True)
    rank2 = jnp.sum(jnp.where(lane == i2, prefix, 0.0), axis=-1, keepdims=True)
    cnt_ref[...] += jnp.sum(onehot, axis=0, keepdims=True)

    rec = jnp.zeros((tm, LANES), F32)
    for slot, val in ((R_E1, i1.astype(F32)), (R_E2, i2.astype(F32)), (R_W1, tw1), (R_W2, tw2),
                      (R_RANK1, rank1), (R_RANK2, rank2)):
        rec = jnp.where(lane == slot, val, rec)
    route_ref[rs, :] = rec
    route_t_ref[:, rs] = rec.T[:SUBLANES, :]


def _oproj(o_list, lse_list, mixed_b, x2d, w_o, g_out_a, g_ffn, w_route, b_route, cnt_in, tm, sub):
    T, D = x2d.shape
    row = lambda i: (i, 0)
    dils = (1,) + STRIDED_DILATIONS
    aspecs = [pl.BlockSpec((tm // d, d * A_WIDTH), row) for d in dils]
    lspecs = [pl.BlockSpec((tm // d, d * LANES), row) for d in dils]
    lspec = pl.BlockSpec((tm, LANES), row)
    dspec = pl.BlockSpec((tm, D), row)
    cspec = pl.BlockSpec((1, LANES), lambda i: (0, 0))
    return pl.pallas_call(
        functools.partial(_oproj_kernel, sub=sub),
        grid=(T // tm,),
        in_specs=[*aspecs, *lspecs, pl.BlockSpec((tm, B_WIDTH), row), dspec,
                  _resident(w_o.shape), _resident((1, A_WIDTH)), _resident((1, D)),
                  _resident(w_route.shape), _resident((1, LANES)), cspec],
        out_specs=[dspec, dspec, lspec, pl.BlockSpec((SUBLANES, tm), lambda i: (0, i)), cspec],
        out_shape=[jax.ShapeDtypeStruct((T, D), F32), jax.ShapeDtypeStruct((T, D), F32),
                   jax.ShapeDtypeStruct((T, LANES), F32), jax.ShapeDtypeStruct((SUBLANES, T), F32),
                   jax.ShapeDtypeStruct((1, LANES), F32)],
        scratch_shapes=[pltpu.VMEM((A_HEADS, tm, HEAD_DIM), F32), pltpu.VMEM((A_HEADS, tm, HEAD_DIM), F32),
                        pltpu.VMEM((tm, LANES), F32), pltpu.VMEM((tm, LANES), F32)],
        compiler_params=_params(("arbitrary",)),
        name="oproj_route",
    )(*o_list, *lse_list, mixed_b, x2d, w_o, g_out_a, g_ffn, w_route, b_route, cnt_in)


def _row_copy(src_ref, src_row, dst_ref, dst_row, sem):
    return pltpu.make_async_copy(src_ref.at[pl.ds(src_row, 1)], dst_ref.at[pl.ds(dst_row, 1)], sem)


ROW1, ROW2 = R_E1, R_E2


def _position_records(route_ts, offsets):
    rec = jnp.concatenate(route_ts, axis=1).astype(jnp.int32)
    seg = jnp.take(offsets, jnp.clip(rec, 0, N_EXPERTS - 1))
    rows = seg + jnp.roll(rec, R_E1 - R_RANK1, axis=0)
    sizes = np.cumsum([r.shape[1] for r in route_ts])[:-1]
    return jnp.split(rows, sizes, axis=1)


def _dispatch_kernel(fill_ref, *rest, tm, tiles_per_src, fill_rows):
    n_src = len(tiles_per_src)
    rec_refs = rest[:n_src]
    u_refs = rest[n_src:2 * n_src]
    xs_ref, sem, zero_ref, fill_sem = rest[2 * n_src:]
    i = pl.program_id(0)

    @pl.when(i == 0)
    def _():
        zero_ref[...] = jnp.zeros_like(zero_ref)

        def tile_copy(k):
            start = pl.multiple_of(fill_ref[k], fill_rows)
            return pltpu.make_async_copy(zero_ref, xs_ref.at[pl.ds(start, fill_rows)], fill_sem)

        for k in range(fill_ref.shape[0]):
            @pl.when(fill_ref[k] >= 0)
            def _():
                tile_copy(k).start()

        for k in range(fill_ref.shape[0]):
            @pl.when(fill_ref[k] >= 0)
            def _():
                tile_copy(k).wait()

    def scatter(rec_ref, u_ref):
        def issue(j, carry):
            for row in (rec_ref[ROW1, j], rec_ref[ROW2, j]):
                _row_copy(u_ref, j, xs_ref, row, sem).start()
            return carry

        lax.fori_loop(0, tm, issue, 0, unroll=8)

        def drain(j, carry):
            _row_copy(u_ref, 0, xs_ref, 0, sem).wait()
            return carry

        lax.fori_loop(0, 2 * tm, drain, 0, unroll=8)

    first = 0
    for rec_ref, u_ref, n in zip(rec_refs, u_refs, tiles_per_src):
        @pl.when((i >= first) & (i < first + n))
        def _():
            scatter(rec_ref, u_ref)

        first += n


def _dispatch(us, recs, fill_starts, n_rows, tm, fill_rows):
    D = us[0].shape[1]
    tiles_per_src = tuple(u.shape[0] // tm for u in us)
    in_specs = [pl.BlockSpec(memory_space=pltpu.SMEM)]
    starts = [sum(tiles_per_src[:k]) for k in range(len(us))]
    for first, n in zip(starts, tiles_per_src):
        in_specs.append(pl.BlockSpec((SUBLANES, tm), lambda i, first=first, n=n: (0, jnp.clip(i - first, 0, n - 1)),
                                     memory_space=pltpu.SMEM))
    for first, n in zip(starts, tiles_per_src):
        in_specs.append(pl.BlockSpec((tm, D), lambda i, first=first, n=n: (jnp.clip(i - first, 0, n - 1), 0)))
    first = sum(tiles_per_src)
    return pl.pallas_call(
        functools.partial(_dispatch_kernel, tm=tm, tiles_per_src=tiles_per_src, fill_rows=fill_rows),
        grid=(first,),
        in_specs=in_specs,
        out_specs=pl.BlockSpec(memory_space=pl.ANY),
        out_shape=jax.ShapeDtypeStruct((n_rows, D), F32),
        scratch_shapes=[pltpu.SemaphoreType.DMA(()), pltpu.VMEM((fill_rows, D), F32),
                        pltpu.SemaphoreType.DMA(())],
        compiler_params=_params(("arbitrary",)),
        name="moe_dispatch",
    )(fill_starts, *recs, *us)


def _experts_kernel(te_ref, used_ref, x_ref, w1_ref, w3_ref, w2_ref, y_ref, w1b_ref, w3b_ref, w2b_ref):
    i = pl.program_id(0)
    cur = jnp.minimum(i, used_ref[0] - 1)
    new_expert = (i == 0) | (te_ref[cur] != te_ref[jnp.maximum(cur - 1, 0)])

    @pl.when(new_expert & (i < used_ref[0]))
    def _():
        w1b_ref[...] = w1_ref[...].astype(BF16)
        w3b_ref[...] = w3_ref[...].astype(BF16)
        w2b_ref[...] = w2_ref[...].astype(BF16)

    @pl.when(i < used_ref[0])
    def _():
        x = x_ref[...].astype(BF16)
        a = jnp.dot(x, w1b_ref[...], preferred_element_type=F32)
        b = jnp.dot(x, w3b_ref[...], preferred_element_type=F32)
        hid = (a * jax.nn.sigmoid(a)) * b
        y_ref[...] = jnp.dot(hid.astype(BF16), w2b_ref[...], preferred_element_type=F32)

    @pl.when(i >= used_ref[0])
    def _():
        y_ref[...] = jnp.zeros_like(y_ref)


def _experts(xs, tile_expert, n_used, w1, w3, w2, tm):
    P, D = xs.shape
    F = w1.shape[2]
    row = lambda i, te, used: (jnp.minimum(i, used[0] - 1), 0)
    wmap = lambda i, te, used: (te[jnp.minimum(i, used[0] - 1)], 0, 0)
    grid_spec = pltpu.PrefetchScalarGridSpec(
        num_scalar_prefetch=2,
        grid=(P // tm,),
        in_specs=[pl.BlockSpec((tm, D), row),
                  pl.BlockSpec((None, D, F), wmap),
                  pl.BlockSpec((None, D, F), wmap),
                  pl.BlockSpec((None, F, D), wmap)],
        out_specs=pl.BlockSpec((tm, D), lambda i, te, used: (i, 0)),
        scratch_shapes=[pltpu.VMEM((D, F), BF16), pltpu.VMEM((D, F), BF16), pltpu.VMEM((F, D), BF16)],
    )
    return pl.pallas_call(
        _experts_kernel,
        grid_spec=grid_spec,
        out_shape=jax.ShapeDtypeStruct((P, D), F32),
        compiler_params=_params(("arbitrary",)),
        name="moe_experts",
    )(tile_expert, n_used, xs, w1, w3, w2)


def _ple_kernel(rec_ref, rec_next_ref, ys_ref, h_ref, route_ref, p_ref, gp_ref, wg_ref, bg_ref,
                wp_ref, gf_ref, o_ref, ybuf_a, ybuf_b, sems, *, tm):
    i = pl.program_id(0)
    half = tm // 2
    bufs = (ybuf_a, ybuf_b)

    def row_copies(rec_ref_, s, j, base):
        return (_row_copy(ys_ref, rec_ref_[ROW1, j], bufs[s].at[0], j - base, sems.at[s]),
                _row_copy(ys_ref, rec_ref_[ROW2, j], bufs[s].at[1], j - base, sems.at[s]))

    def drain(s):
        def wait(j, carry):
            _row_copy(ys_ref, 0, bufs[s].at[0], 0, sems.at[s]).wait()
            return carry

        lax.fori_loop(0, 2 * half, wait, 0, unroll=8)

    n_chunks = 4
    width = h_ref.shape[1] // n_chunks

    def compute(s, rows, requests):
        route = route_ref[rows, :]
        h = h_ref[rows, :] + route[:, R_W1:R_W1 + 1] * bufs[s][0] + route[:, R_W2:R_W2 + 1] * bufs[s][1]
        u = _rms(h, gp_ref[...]).astype(BF16)
        p = p_ref[rows, :].astype(BF16)
        per = len(requests) // n_chunks
        parts = []
        for c in range(n_chunks):
            for request in requests[c * per:(c + 1) * per]:
                for copy in row_copies(*request):
                    copy.start()
            cols = slice(c * width, (c + 1) * width)
            gate = jax.nn.sigmoid(jnp.dot(u, wg_ref[:, cols], preferred_element_type=F32) + bg_ref[:, cols])
            proj = jnp.dot(p, wp_ref[:, cols], preferred_element_type=F32)
            parts.append(h[:, cols] + gate * proj)
        o_ref[rows, :] = _rms(jnp.concatenate(parts, axis=-1), gf_ref[...])

    @pl.when(i == 0)
    def _():
        def issue(j, carry):
            for copy in row_copies(rec_ref, 0, j, 0):
                copy.start()
            return carry

        lax.fori_loop(0, half, issue, 0, unroll=8)

    drain(0)
    compute(0, slice(0, half), [(rec_ref, 1, j, half) for j in range(half, tm)])
    drain(1)
    compute(1, slice(half, tm), [(rec_next_ref, 0, j, 0) for j in range(half)])

    @pl.when(i + 1 == pl.num_programs(0))
    def _():
        drain(0)


def _ple(rec, ys, h, route, p2d, g_ple, w_gate, b_gate, w_proj, g_final, tm):
    T, D = h.shape
    P = p2d.shape[1]
    row = lambda i: (i, 0)
    n = T // tm
    return pl.pallas_call(
        functools.partial(_ple_kernel, tm=tm),
        grid=(n,),
        in_specs=[pl.BlockSpec((SUBLANES, tm), lambda i: (0, i), memory_space=pltpu.SMEM),
                  pl.BlockSpec((SUBLANES, tm), lambda i: (0, jnp.minimum(i + 1, n - 1)),
                               memory_space=pltpu.SMEM),
                  pl.BlockSpec(memory_space=pl.ANY),
                  pl.BlockSpec((tm, D), row), pl.BlockSpec((tm, LANES), row), pl.BlockSpec((tm, P), row),
                  _resident((1, D)), _resident(w_gate.shape), _resident((1, D)), _resident(w_proj.shape),
                  _resident((1, D))],
        out_specs=pl.BlockSpec((tm, D), row),
        out_shape=jax.ShapeDtypeStruct((T, D), F32),
        scratch_shapes=[pltpu.VMEM((2, tm // 2, D), F32), pltpu.VMEM((2, tm // 2, D), F32),
                        pltpu.SemaphoreType.DMA((2,))],
        compiler_params=_params(("arbitrary",)),
        name="ple_final",
    )(rec, rec, ys, h, route, p2d, g_ple, w_gate, b_gate, w_proj, g_final)


def _rope_tables(seq):
    half = ROPE_DIM // 2
    inv = ROPE_THETA ** (-(jnp.arange(half, dtype=F32) * 2.0) / ROPE_DIM)
    ang = jnp.arange(seq, dtype=F32)[:, None] * inv[None, :]
    cos, sin = jnp.cos(ang), jnp.sin(ang)
    zeros = jnp.zeros((seq, LANES - ROPE_DIM), F32)
    z16 = jnp.zeros((seq, half), F32)
    cos_t = jnp.concatenate([cos, cos, jnp.ones((seq, LANES - ROPE_DIM), F32)], axis=-1)
    sa_t = jnp.concatenate([z16, sin, zeros], axis=-1)
    sb_t = jnp.concatenate([-sin, z16, zeros], axis=-1)
    return cos_t, sa_t, sb_t


MOE_TILE = 256
OPROJ_TILE = 512
PLE_TILE = 512
DISPATCH_TILE = 1024


def _mixer(x, wts, cnt_in):
    batch, seq, D = x.shape
    T = batch * seq
    x2d = x.reshape(T, D)
    cos, sa, sb = _rope_tables(seq)
    qa, ka, va, qb, kb, vb, *strided = _qkv(x2d, wts["g_attn"], wts["w_in"], cos, sa, sb, seq, tm=256)
    n_dil = len(STRIDED_DILATIONS)
    qkv_by_dil = {1: (qa, ka, va)}
    for n, d in enumerate(STRIDED_DILATIONS):
        qkv_by_dil[d] = tuple(strided[i * n_dil + n] for i in range(3))
    o_list, lse_list = [], []
    for window, dil in DILATED_CONFIGS:
        o, lse = _attn_a(*qkv_by_dil[dil], batch, seq, dil, window // (2 * dil))
        o_list.append(o)
        lse_list.append(lse)
    mixed_b = _attn_b(qb, kb, vb, wts["sink"], wts["g_out_b"], batch, seq)
    return _oproj(o_list, lse_list, mixed_b, x2d, wts["w_o"], wts["g_out_a"], wts["g_ffn"],
                  wts["w_route"], wts["b_route"], cnt_in, tm=OPROJ_TILE, sub=OPROJ_TILE // 2)


def _sorted_layout(counts, n_tiles, tm):
    cnt = counts[0, :N_EXPERTS].astype(jnp.int32)
    tiles = (cnt + tm - 1) // tm
    ends = jnp.cumsum(tiles)
    offsets = (ends - tiles) * tm
    tile_expert = jnp.sum(jnp.arange(n_tiles, dtype=jnp.int32)[:, None] >= ends[None, :], axis=1)
    tile_expert = jnp.minimum(tile_expert, N_EXPERTS - 1).astype(jnp.int32)
    n_used = ends[-1]
    last_tile_start = jnp.where(tiles > 0, (ends - 1) * tm, -1)
    tail = n_used + jnp.arange(N_EXPERTS, dtype=jnp.int32)
    tail_start = jnp.where(tail < n_tiles, tail * tm, -1)
    fill_starts = jnp.concatenate([last_tile_start, tail_start]).astype(jnp.int32)
    return offsets, tile_expert, n_used.reshape(1).astype(jnp.int32), fill_starts


def kernel(x_prompt, x_sample, p_prompt, p_sample, g_attn, w_in, g_out_a, g_out_b, sink, w_o, g_ffn, w_group, b_group, w_router, b_router, w1, w3, w2, g_ple, w_ple_gate, b_ple_gate, w_ple_proj, g_final):
    depth = w_in.shape[0]
    assert depth == 1
    l = 0
    D = w_in.shape[1]
    pad = LANES - N_EXPERTS - N_GROUPS
    w_route = jnp.pad(jnp.concatenate([w_router[l], w_group[l]], axis=-1), ((0, 0), (0, pad)))
    w_route_hi = w_route.astype(BF16)
    w_route_lo = (w_route - w_route_hi.astype(F32)).astype(BF16)
    wts = {
        "g_attn": g_attn[l].reshape(1, D),
        "w_in": w_in[l].astype(BF16),
        "g_out_a": g_out_a[l].reshape(1, A_WIDTH),
        "g_out_b": g_out_b[l].reshape(1, B_WIDTH),
        "sink": sink[l],
        "w_o": w_o[l].astype(BF16),
        "g_ffn": g_ffn[l].reshape(1, D),
        "w_route": jnp.concatenate([w_route_hi, w_route_lo], axis=-1),
        "b_route": jnp.pad(jnp.concatenate([b_router[l], b_group[l]]), (0, pad)).reshape(1, LANES),
        "w1": w1[l],
        "w3": w3[l],
        "w2": w2[l],
        "g_ple": g_ple[l].reshape(1, D),
        "w_ple_gate": w_ple_gate[l].astype(BF16),
        "b_ple_gate": b_ple_gate[l].reshape(1, D),
        "w_ple_proj": w_ple_proj[l].astype(BF16),
        "g_final": g_final.reshape(1, D),
    }
    xs_list = (x_prompt, x_sample)
    ps_list = (p_prompt[l], p_sample[l])

    counts = jnp.zeros((1, LANES), F32)
    mixed = []
    route_ts = []
    for x in xs_list:
        h1, u2, route, route_t, counts = _mixer(x, wts, counts)
        mixed.append((h1, u2, route))
        route_ts.append(route_t)

    n_pairs = 2 * sum(x.shape[0] * x.shape[1] for x in xs_list)
    n_tiles = n_pairs // MOE_TILE + N_EXPERTS
    offsets, tile_expert, n_used, fill_starts = _sorted_layout(counts, n_tiles, MOE_TILE)
    recs = _position_records(route_ts, offsets)
    xs = _dispatch([u2 for _, u2, _ in mixed], recs, fill_starts, n_tiles * MOE_TILE,
                   DISPATCH_TILE, MOE_TILE)
    ys = _experts(xs, tile_expert, n_used, wts["w1"], wts["w3"], wts["w2"], MOE_TILE)

    outs = []
    for x, p, (h1, u2, route), rec in zip(xs_list, ps_list, mixed, recs):
        y = _ple(rec, ys, h1, route, p.reshape(h1.shape[0], -1),
                 wts["g_ple"], wts["w_ple_gate"], wts["b_ple_gate"], wts["w_ple_proj"], wts["g_final"],
                 PLE_TILE)
        outs.append(y.reshape(x.shape))
    return tuple(outs)
```

```python
import functools

import jax
import jax.numpy as jnp
import numpy as np
from jax import lax
from jax.experimental import pallas as pl
from jax.experimental.pallas import tpu as pltpu

HEAD_DIM = 128
A_HEADS = 8
A_WIDTH = A_HEADS * HEAD_DIM
B_HEADS = 8
B_KV_HEADS = 2
B_GROUP = B_HEADS // B_KV_HEADS
B_WIDTH = B_HEADS * HEAD_DIM
B_KV_WIDTH = B_KV_HEADS * HEAD_DIM
DILATED_CONFIGS = ((128, 1), (512, 4), (2048, 16))
B_RADIUS = 128
ROPE_THETA = 500000.0
ROPE_DIM = HEAD_DIM // 4
N_GROUPS = 4
EXPERTS_PER_GROUP = 8
N_EXPERTS = N_GROUPS * EXPERTS_PER_GROUP
RMS_EPS = 1e-6
NEG_INF = -1e30

LANES = 128
SUBLANES = 8
VMEM_LIMIT = 56 * 1024 * 1024

F32 = jnp.float32
BF16 = jnp.bfloat16


def _resident(shape):
    nd = len(shape)
    return pl.BlockSpec(shape, lambda *_: (0,) * nd, pipeline_mode=pl.Buffered(1))


def _params(sem):
    return pltpu.CompilerParams(dimension_semantics=sem, vmem_limit_bytes=VMEM_LIMIT)


def _rms(x, g):
    return x * lax.rsqrt(jnp.mean(x * x, axis=-1, keepdims=True) + RMS_EPS) * g


STRIDED_DILATIONS = tuple(d for _, d in DILATED_CONFIGS if d > 1)


def _qkv_kernel(x_ref, g_ref, w_ref, cos_ref, sa_ref, sb_ref, *rest):
    n_dil = len(STRIDED_DILATIONS)
    perm_refs = rest[:n_dil]
    qa_ref, ka_ref, va_ref, qb_ref, kb_ref, vb_ref = rest[n_dil:n_dil + 6]
    dil_refs = [rest[n_dil + 6 + i * n_dil:n_dil + 6 + (i + 1) * n_dil] for i in range(3)]
    tm = x_ref.shape[0]
    u = _rms(x_ref[...], g_ref[...]).astype(BF16)
    cos = cos_ref[...]
    sa = sa_ref[...]
    sb = sb_ref[...]
    half = ROPE_DIM // 2
    scale = HEAD_DIM ** -0.5

    def rope(t):
        return t * cos + pltpu.roll(t, half, 1) * sa + pltpu.roll(t, LANES - half, 1) * sb

    sections = (
        (qa_ref, 0, A_HEADS, True, scale, 0),
        (ka_ref, A_WIDTH, A_HEADS, True, None, 1),
        (va_ref, 2 * A_WIDTH, A_HEADS, False, None, 2),
        (qb_ref, 3 * A_WIDTH, B_HEADS, True, scale, None),
        (kb_ref, 3 * A_WIDTH + B_WIDTH, B_KV_HEADS, True, None, None),
        (vb_ref, 3 * A_WIDTH + B_WIDTH + B_KV_WIDTH, B_KV_HEADS, False, None, None),
    )
    for out_ref, c0, heads, rotary, sc, a_idx in sections:
        for h0 in range(0, heads, 4):
            nh = min(4, heads - h0)
            lo = c0 + h0 * HEAD_DIM
            acc = jnp.dot(u, w_ref[:, lo:lo + nh * HEAD_DIM], preferred_element_type=F32)
            for j in range(nh):
                t = acc[:, j * HEAD_DIM:(j + 1) * HEAD_DIM]
                if rotary:
                    t = rope(t)
                if sc is not None:
                    t = t * sc
                cols = slice((h0 + j) * HEAD_DIM, (h0 + j + 1) * HEAD_DIM)
                out_ref[:, cols] = t.astype(BF16)
        if a_idx is not None:
            for d, perm_ref, dref in zip(STRIDED_DILATIONS, perm_refs, dil_refs[a_idx]):
                n = tm // d
                permuted = jnp.dot(perm_ref[...], out_ref[...], preferred_element_type=F32).astype(BF16)
                for r in range(d):
                    dref[:, r * A_WIDTH:(r + 1) * A_WIDTH] = permuted[r * n:(r + 1) * n, :]


def _qkv(x2d, g, w_in, cos, sa, sb, seq, tm):
    T, D = x2d.shape
    n_pos = seq // tm
    row = lambda i: (i, 0)
    tab = pl.BlockSpec((tm, LANES), lambda i: (i % n_pos, 0))
    widths = (A_WIDTH, A_WIDTH, A_WIDTH, B_WIDTH, B_KV_WIDTH, B_KV_WIDTH)
    out_specs = [pl.BlockSpec((tm, w), row) for w in widths]
    out_shape = [jax.ShapeDtypeStruct((T, w), BF16) for w in widths]
    for _ in range(3):
        for d in STRIDED_DILATIONS:
            out_specs.append(pl.BlockSpec((tm // d, d * A_WIDTH), row))
            out_shape.append(jax.ShapeDtypeStruct((T // d, d * A_WIDTH), BF16))
    perms = [_residue_permutation(tm, d) for d in STRIDED_DILATIONS]
    return pl.pallas_call(
        _qkv_kernel,
        grid=(T // tm,),
        in_specs=[pl.BlockSpec((tm, D), row), _resident((1, D)), _resident(w_in.shape), tab, tab, tab,
                  *[_resident((tm, tm)) for _ in perms]],
        out_specs=out_specs,
        out_shape=out_shape,
        compiler_params=_params(("parallel",)),
        name="qkv",
    )(x2d, g, w_in, cos, sa, sb, *perms)


def _residue_permutation(n, d):
    dst = np.arange(n)
    src = (dst % (n // d)) * d + dst // (n // d)
    p = np.zeros((n, n), np.float32)
    p[dst, src] = 1.0
    return jnp.asarray(p, BF16)


ATTN_SUB = 128
ATTN_BLOCK = 1024


def _band_mask(first_q, sub, radius, length):
    shape = (sub, sub + 2 * radius)
    row = lax.broadcasted_iota(jnp.int32, shape, 0)
    col = lax.broadcasted_iota(jnp.int32, shape, 1)
    kpos = first_q - radius + col
    return (jnp.abs(col - radius - row) <= radius) & (kpos >= 0) & (kpos < length)


def _fill_window(ext_ref, prev_ref, cur_ref, next_ref, radius):
    n = cur_ref.shape[0]
    ext_ref[0:radius, :] = prev_ref[...]
    ext_ref[radius:radius + n, :] = cur_ref[...]
    ext_ref[radius + n:, :] = next_ref[...]


_NT = (((1,), (1,)), ((), ()))


def _attn_a_kernel(q_ref, kp_ref, kc_ref, kn_ref, vp_ref, vc_ref, vn_ref, o_ref, lse_ref, kx_ref, vx_ref,
                   *, length, lq, radius, n_res):
    sub = min(ATTN_SUB, lq)
    _fill_window(kx_ref, kp_ref, kc_ref, kn_ref, radius)
    _fill_window(vx_ref, vp_ref, vc_ref, vn_ref, radius)
    lane = lax.broadcasted_iota(jnp.int32, (sub, LANES), 1)
    for j in range(lq // sub):
        valid = _band_mask(pl.program_id(2) * lq + j * sub, sub, radius, length)
        q_rows = slice(j * sub, (j + 1) * sub)
        k_rows = slice(j * sub, (j + 1) * sub + 2 * radius)
        for res in range(n_res):
            cols = [slice(res * A_WIDTH + h * HEAD_DIM, res * A_WIDTH + (h + 1) * HEAD_DIM)
                    for h in range(A_HEADS)]
            q = jnp.stack([q_ref[q_rows, sl] for sl in cols], axis=0)
            k = jnp.stack([kx_ref[k_rows, sl] for sl in cols], axis=0)
            v = jnp.stack([vx_ref[k_rows, sl] for sl in cols], axis=0)
            s = jnp.einsum("hqd,hkd->hqk", q, k, preferred_element_type=F32)
            s = jnp.where(valid[None], s, NEG_INF)
            m = jnp.max(s, axis=-1, keepdims=True)
            p = jnp.exp(s - m)
            l = jnp.sum(p, axis=-1, keepdims=True)
            o = jnp.einsum("hqk,hkd->hqd", p.astype(BF16), v, preferred_element_type=F32) / l
            lse = m + jnp.log(l)
            lse_all = jnp.zeros((sub, LANES), F32)
            for h, sl in enumerate(cols):
                o_ref[q_rows, sl] = o[h].astype(o_ref.dtype)
                lse_all = jnp.where(lane == h, lse[h], lse_all)
            lse_ref[q_rows, res * LANES:(res + 1) * LANES] = lse_all


def _attn_a(qa, ka, va, batch, seq, dil, radius):
    ls = seq // dil
    lq = min(ATTN_BLOCK, ls)
    n_res = min(dil, ATTN_BLOCK // lq)
    per = lq // radius
    n_halo = ls // radius
    view = lambda t: t.reshape(batch, ls, dil * A_WIDTH)
    w = n_res * A_WIDTH
    cur = pl.BlockSpec((None, lq, w), lambda b, r, i: (b, i, r))
    prv = pl.BlockSpec((None, radius, w), lambda b, r, i: (b, jnp.maximum(i * per - 1, 0), r))
    nxt = pl.BlockSpec((None, radius, w), lambda b, r, i: (b, jnp.minimum((i + 1) * per, n_halo - 1), r))
    o, lse = pl.pallas_call(
        functools.partial(_attn_a_kernel, length=ls, lq=lq, radius=radius, n_res=n_res),
        grid=(batch, dil // n_res, ls // lq),
        in_specs=[cur, prv, cur, nxt, prv, cur, nxt],
        out_specs=[cur, pl.BlockSpec((None, lq, n_res * LANES), lambda b, r, i: (b, i, r))],
        out_shape=[jax.ShapeDtypeStruct((batch, ls, dil * A_WIDTH), BF16),
                   jax.ShapeDtypeStruct((batch, ls, dil * LANES), F32)],
        scratch_shapes=[pltpu.VMEM((lq + 2 * radius, w), BF16), pltpu.VMEM((lq + 2 * radius, w), BF16)],
        compiler_params=_params(("parallel", "parallel", "parallel")),
        name=f"attn_a_d{dil}",
    )(view(qa), view(ka), view(ka), view(ka), view(va), view(va), view(va))
    return o.reshape(batch * ls, dil * A_WIDTH), lse.reshape(batch * ls, dil * LANES)


def _attn_b_kernel(sink_ref, q_ref, kp_ref, kc_ref, kn_ref, vp_ref, vc_ref, vn_ref, g_ref, o_ref,
                   kx_ref, vx_ref, *, length, lq, radius):
    sub = ATTN_SUB
    _fill_window(kx_ref, kp_ref, kc_ref, kn_ref, radius)
    _fill_window(vx_ref, vp_ref, vc_ref, vn_ref, radius)
    for j in range(lq // sub):
        valid = _band_mask(pl.program_id(1) * lq + j * sub, sub, radius, length)
        q_rows = slice(j * sub, (j + 1) * sub)
        k_rows = slice(j * sub, (j + 1) * sub + 2 * radius)
        outs = []
        for g in range(B_KV_HEADS):
            kv = slice(g * HEAD_DIM, (g + 1) * HEAD_DIM)
            heads = range(g * B_GROUP, (g + 1) * B_GROUP)
            q = jnp.concatenate([q_ref[q_rows, h * HEAD_DIM:(h + 1) * HEAD_DIM] for h in heads], axis=0)
            s = lax.dot_general(q, kx_ref[k_rows, kv], _NT, preferred_element_type=F32)
            s = jnp.where(valid[None], s.reshape(B_GROUP, sub, sub + 2 * radius), NEG_INF)
            head = lax.broadcasted_iota(jnp.int32, (B_GROUP, 1, 1), 0)
            sink = jnp.zeros((B_GROUP, 1, 1), F32)
            for n, h in enumerate(heads):
                sink = jnp.where(head == n, sink_ref[h], sink)
            m = jnp.maximum(jnp.max(s, axis=-1, keepdims=True), sink)
            p = jnp.exp(s - m)
            l = jnp.sum(p, axis=-1, keepdims=True) + jnp.exp(sink - m)
            o = jnp.dot(p.reshape(B_GROUP * sub, sub + 2 * radius).astype(BF16), vx_ref[k_rows, kv],
                        preferred_element_type=F32).reshape(B_GROUP, sub, HEAD_DIM) / l
            outs.extend(o[n] for n in range(B_GROUP))
        o_ref[q_rows, :] = _rms(jnp.concatenate(outs, axis=-1), g_ref[...]).astype(o_ref.dtype)


def _attn_b(qb, kb, vb, sink, g_out_b, batch, seq):
    radius = B_RADIUS
    lq = min(ATTN_BLOCK, seq)
    per = lq // radius
    n_halo = seq // radius
    q3 = qb.reshape(batch, seq, B_WIDTH)
    k3 = kb.reshape(batch, seq, B_KV_WIDTH)
    v3 = vb.reshape(batch, seq, B_KV_WIDTH)
    qspec = pl.BlockSpec((None, lq, B_WIDTH), lambda b, i: (b, i, 0))
    cur = pl.BlockSpec((None, lq, B_KV_WIDTH), lambda b, i: (b, i, 0))
    prv = pl.BlockSpec((None, radius, B_KV_WIDTH), lambda b, i: (b, jnp.maximum(i * per - 1, 0), 0))
    nxt = pl.BlockSpec((None, radius, B_KV_WIDTH),
                       lambda b, i: (b, jnp.minimum((i + 1) * per, n_halo - 1), 0))
    ext = pltpu.VMEM((lq + 2 * radius, B_KV_WIDTH), BF16)
    out = pl.pallas_call(
        functools.partial(_attn_b_kernel, length=seq, lq=lq, radius=radius),
        grid=(batch, seq // lq),
        in_specs=[pl.BlockSpec(memory_space=pltpu.SMEM), qspec, prv, cur, nxt, prv, cur, nxt,
                  _resident((1, B_WIDTH))],
        out_specs=qspec,
        out_shape=jax.ShapeDtypeStruct((batch, seq, B_WIDTH), BF16),
        scratch_shapes=[ext, ext],
        compiler_params=_params(("parallel", "parallel")),
        name="attn_b",
    )(sink, q3, k3, k3, k3, v3, v3, v3, g_out_b)
    return out.reshape(batch * seq, B_WIDTH)


R_E1, R_E2, R_W1, R_W2, R_RANK1, R_RANK2 = range(6)


def _oproj_kernel(o1_ref, o4_ref, o16_ref, l1_ref, l4_ref, l16_ref, mb_ref, x_ref, wo_ref,
                  ga_ref, gf_ref, wr_ref, br_ref, cnt_in_ref, h_ref, u_ref, route_ref, route_t_ref, cnt_ref,
                  o4n_ref, o16n_ref, l4n_ref, l16n_ref, *, sub):
    tm = x_ref.shape[0]

    @pl.when(pl.program_id(0) == 0)
    def _():
        cnt_ref[...] = cnt_in_ref[...]

    for d, o_src, o_dst, l_src, l_dst in ((STRIDED_DILATIONS[0], o4_ref, o4n_ref, l4_ref, l4n_ref),
                                          (STRIDED_DILATIONS[1], o16_ref, o16n_ref, l16_ref, l16n_ref)):
        for r in range(d):
            rows = pl.ds(r, tm // d, stride=d)
            for h in range(A_HEADS):
                lo = r * A_WIDTH + h * HEAD_DIM
                o_dst[h, rows, :] = o_src[:, lo:lo + HEAD_DIM].astype(F32)
            l_dst[rows, :] = l_src[:, r * LANES:(r + 1) * LANES]

    for c in range(tm // sub):
        _oproj_rows(slice(c * sub, (c + 1) * sub), o1_ref, o4n_ref, o16n_ref, l1_ref, l4n_ref, l16n_ref,
                    mb_ref, x_ref, wo_ref, ga_ref, gf_ref, wr_ref, br_ref, h_ref, u_ref, route_ref,
                    route_t_ref, cnt_ref)


def _oproj_rows(rs, o1_ref, o2_ref, o3_ref, l1_ref, l2_ref, l3_ref, mb_ref, x_ref, wo_ref,
                ga_ref, gf_ref, wr_ref, br_ref, h_ref, u_ref, route_ref, route_t_ref, cnt_ref):
    tm = rs.stop - rs.start
    l1, l2, l3 = l1_ref[rs, :], l2_ref[rs, :], l3_ref[rs, :]
    mx = jnp.maximum(jnp.maximum(l1, l2), l3)
    e1, e2, e3 = jnp.exp(l1 - mx), jnp.exp(l2 - mx), jnp.exp(l3 - mx)
    inv = 1.0 / (e1 + e2 + e3)
    w1, w2, w3 = e1 * inv, e2 * inv, e3 * inv
    parts = []
    for h in range(A_HEADS):
        sl = slice(h * HEAD_DIM, (h + 1) * HEAD_DIM)
        bc = lambda w: jnp.broadcast_to(w[:, h:h + 1], (tm, HEAD_DIM))
        parts.append(bc(w1) * o1_ref[rs, sl].astype(F32) + bc(w2) * o2_ref[h, rs, :]
                     + bc(w3) * o3_ref[h, rs, :])
    mixed_a = _rms(jnp.concatenate(parts, axis=-1), ga_ref[...]).astype(BF16)
    h = (x_ref[rs, :]
         + jnp.dot(mixed_a, wo_ref[:A_WIDTH, :], preferred_element_type=F32)
         + jnp.dot(mb_ref[rs, :], wo_ref[A_WIDTH:, :], preferred_element_type=F32))
    h_ref[rs, :] = h
    u = _rms(h, gf_ref[...])
    u_ref[rs, :] = u

    u_hi = u.astype(BF16)
    u_lo = (u - u_hi.astype(F32)).astype(BF16)
    z = (jnp.dot(u_hi, wr_ref[...], preferred_element_type=F32)
         + jnp.dot(u_lo, wr_ref[...], preferred_element_type=F32))
    logits = z[:, :LANES] + z[:, LANES:] + br_ref[...]
    lane = lax.broadcasted_iota(jnp.int32, (tm, LANES), 1)
    big = jnp.int32(LANES)
    is_group = (lane >= N_EXPERTS) & (lane < N_EXPERTS + N_GROUPS)
    gmax = jnp.max(jnp.where(is_group, logits, -jnp.inf), axis=-1, keepdims=True)
    gidx = jnp.min(jnp.where(is_group & (logits == gmax), lane - N_EXPERTS, big), axis=-1, keepdims=True)
    ggate = 1.0 / jnp.sum(jnp.where(is_group, jnp.exp(logits - gmax), 0.0), axis=-1, keepdims=True)
    in_group = (lane >= gidx * EXPERTS_PER_GROUP) & (lane < (gidx + 1) * EXPERTS_PER_GROUP)
    v1 = jnp.max(jnp.where(in_group, logits, -jnp.inf), axis=-1, keepdims=True)
    i1 = jnp.min(jnp.where(in_group & (logits == v1), lane, big), axis=-1, keepdims=True)
    rest = in_group & (lane != i1)
    v2 = jnp.max(jnp.where(rest, logits, -jnp.inf), axis=-1, keepdims=True)
    i2 = jnp.min(jnp.where(rest & (logits == v2), lane, big), axis=-1, keepdims=True)
    t = jnp.exp(v2 - v1)
    tw1 = ggate / (1.0 + t)
    tw2 = ggate * t / (1.0 + t)

    onehot = ((lane == i1) | (lane == i2)).astype(F32)
    r_i = lax.broadcasted_iota(jnp.int32, (tm, tm), 0)
    c_i = lax.broadcasted_iota(jnp.int32, (tm, tm), 1)
    before = (c_i < r_i).astype(BF16)
    prefix = jnp.dot(before, onehot.astype(BF16), preferred_element_type=F32) + cnt_ref[...]
    rank1 = jnp.sum(jnp.where(lane == i1, prefix, 0.0), axis=-1, keepdims=True)
    rank2 = jnp.sum(jnp.where(lane == i2, prefix, 0.0), axis=-1, keepdims=True)
    cnt_ref[...] += jnp.sum(onehot, axis=0, keepdims=True)

    rec = jnp.zeros((tm, LANES), F32)
    for slot, val in ((R_E1, i1.astype(F32)), (R_E2, i2.astype(F32)), (R_W1, tw1), (R_W2, tw2),
                      (R_RANK1, rank1), (R_RANK2, rank2)):
        rec = jnp.where(lane == slot, val, rec)
    route_ref[rs, :] = rec
    route_t_ref[:, rs] = rec.T[:SUBLANES, :]


def _oproj(o_list, lse_list, mixed_b, x2d, w_o, g_out_a, g_ffn, w_route, b_route, cnt_in, tm, sub):
    T, D = x2d.shape
    row = lambda i: (i, 0)
    dils = (1,) + STRIDED_DILATIONS
    aspecs = [pl.BlockSpec((tm // d, d * A_WIDTH), row) for d in dils]
    lspecs = [pl.BlockSpec((tm // d, d * LANES), row) for d in dils]
    lspec = pl.BlockSpec((tm, LANES), row)
    dspec = pl.BlockSpec((tm, D), row)
    cspec = pl.BlockSpec((1, LANES), lambda i: (0, 0))
    return pl.pallas_call(
        functools.partial(_oproj_kernel, sub=sub),
        grid=(T // tm,),
        in_specs=[*aspecs, *lspecs, pl.BlockSpec((tm, B_WIDTH), row), dspec,
                  _resident(w_o.shape), _resident((1, A_WIDTH)), _resident((1, D)),
                  _resident(w_route.shape), _resident((1, LANES)), cspec],
        out_specs=[dspec, dspec, lspec, pl.BlockSpec((SUBLANES, tm), lambda i: (0, i)), cspec],
        out_shape=[jax.ShapeDtypeStruct((T, D), F32), jax.ShapeDtypeStruct((T, D), F32),
                   jax.ShapeDtypeStruct((T, LANES), F32), jax.ShapeDtypeStruct((SUBLANES, T), F32),
                   jax.ShapeDtypeStruct((1, LANES), F32)],
        scratch_shapes=[pltpu.VMEM((A_HEADS, tm, HEAD_DIM), F32), pltpu.VMEM((A_HEADS, tm, HEAD_DIM), F32),
                        pltpu.VMEM((tm, LANES), F32), pltpu.VMEM((tm, LANES), F32)],
        compiler_params=_params(("arbitrary",)),
        name="oproj_route",
    )(*o_list, *lse_list, mixed_b, x2d, w_o, g_out_a, g_ffn, w_route, b_route, cnt_in)


def _row_copy(src_ref, src_row, dst_ref, dst_row, sem):
    return pltpu.make_async_copy(src_ref.at[pl.ds(src_row, 1)], dst_ref.at[pl.ds(dst_row, 1)], sem)


ROW1, ROW2 = R_E1, R_E2


def _position_records(route_ts, offsets):
    rec = jnp.concatenate(route_ts, axis=1).astype(jnp.int32)
    expert = jnp.stack([rec[R_E1], rec[R_E2]])
    rank = jnp.stack([rec[R_RANK1], rec[R_RANK2]])
    seg = jnp.zeros_like(expert)
    for e in range(N_EXPERTS):
        seg = jnp.where(expert == e, offsets[e], seg)
    rows = jnp.concatenate([seg + rank, jnp.zeros((SUBLANES - 2, rec.shape[1]), jnp.int32)], axis=0)
    sizes = np.cumsum([r.shape[1] for r in route_ts])[:-1]
    return jnp.split(rows, sizes, axis=1)


def _dispatch_kernel(fill_ref, *rest, tm, tiles_per_src, fill_rows):
    n_src = len(tiles_per_src)
    rec_refs = rest[:n_src]
    u_refs = rest[n_src:2 * n_src]
    xs_ref, sem, zero_ref, fill_sem = rest[2 * n_src:]
    i = pl.program_id(0)

    @pl.when(i == 0)
    def _():
        zero_ref[...] = jnp.zeros_like(zero_ref)

        def tile_copy(k):
            start = pl.multiple_of(fill_ref[k], fill_rows)
            return pltpu.make_async_copy(zero_ref, xs_ref.at[pl.ds(start, fill_rows)], fill_sem)

        for k in range(fill_ref.shape[0]):
            @pl.when(fill_ref[k] >= 0)
            def _():
                tile_copy(k).start()

        for k in range(fill_ref.shape[0]):
            @pl.when(fill_ref[k] >= 0)
            def _():
                tile_copy(k).wait()

    def scatter(rec_ref, u_ref):
        def issue(j, carry):
            for row in (rec_ref[ROW1, j], rec_ref[ROW2, j]):
                _row_copy(u_ref, j, xs_ref, row, sem).start()
            return carry

        lax.fori_loop(0, tm, issue, 0, unroll=8)

        def drain(j, carry):
            _row_copy(u_ref, 0, xs_ref, 0, sem).wait()
            return carry

        lax.fori_loop(0, 2 * tm, drain, 0, unroll=8)

    first = 0
    for rec_ref, u_ref, n in zip(rec_refs, u_refs, tiles_per_src):
        @pl.when((i >= first) & (i < first + n))
        def _():
            scatter(rec_ref, u_ref)

        first += n


def _dispatch(us, recs, fill_starts, n_rows, tm, fill_rows):
    D = us[0].shape[1]
    tiles_per_src = tuple(u.shape[0] // tm for u in us)
    in_specs = [pl.BlockSpec(memory_space=pltpu.SMEM)]
    starts = [sum(tiles_per_src[:k]) for k in range(len(us))]
    for first, n in zip(starts, tiles_per_src):
        in_specs.append(pl.BlockSpec((SUBLANES, tm), lambda i, first=first, n=n: (0, jnp.clip(i - first, 0, n - 1)),
                                     memory_space=pltpu.SMEM))
    for first, n in zip(starts, tiles_per_src):
        in_specs.append(pl.BlockSpec((tm, D), lambda i, first=first, n=n: (jnp.clip(i - first, 0, n - 1), 0)))
    first = sum(tiles_per_src)
    return pl.pallas_call(
        functools.partial(_dispatch_kernel, tm=tm, tiles_per_src=tiles_per_src, fill_rows=fill_rows),
        grid=(first,),
        in_specs=in_specs,
        out_specs=pl.BlockSpec(memory_space=pl.ANY),
        out_shape=jax.ShapeDtypeStruct((n_rows, D), F32),
        scratch_shapes=[pltpu.SemaphoreType.DMA(()), pltpu.VMEM((fill_rows, D), F32),
                        pltpu.SemaphoreType.DMA(())],
        compiler_params=_params(("arbitrary",)),
        name="moe_dispatch",
    )(fill_starts, *recs, *us)


def _experts_kernel(te_ref, used_ref, x_ref, w1_ref, w3_ref, w2_ref, y_ref, w1b_ref, w3b_ref, w2b_ref):
    i = pl.program_id(0)
    cur = jnp.minimum(i, used_ref[0] - 1)
    new_expert = (i == 0) | (te_ref[cur] != te_ref[jnp.maximum(cur - 1, 0)])

    @pl.when(new_expert & (i < used_ref[0]))
    def _():
        w1b_ref[...] = w1_ref[...].astype(BF16)
        w3b_ref[...] = w3_ref[...].astype(BF16)
        w2b_ref[...] = w2_ref[...].astype(BF16)

    @pl.when(i < used_ref[0])
    def _():
        x = x_ref[...].astype(BF16)
        a = jnp.dot(x, w1b_ref[...], preferred_element_type=F32)
        b = jnp.dot(x, w3b_ref[...], preferred_element_type=F32)
        hid = (a * jax.nn.sigmoid(a)) * b
        y_ref[...] = jnp.dot(hid.astype(BF16), w2b_ref[...], preferred_element_type=F32)

    @pl.when(i >= used_ref[0])
    def _():
        y_ref[...] = jnp.zeros_like(y_ref)


def _experts(xs, tile_expert, n_used, w1, w3, w2, tm):
    P, D = xs.shape
    F = w1.shape[2]
    row = lambda i, te, used: (jnp.minimum(i, used[0] - 1), 0)
    wmap = lambda i, te, used: (te[jnp.minimum(i, used[0] - 1)], 0, 0)
    grid_spec = pltpu.PrefetchScalarGridSpec(
        num_scalar_prefetch=2,
        grid=(P // tm,),
        in_specs=[pl.BlockSpec((tm, D), row),
                  pl.BlockSpec((None, D, F), wmap),
                  pl.BlockSpec((None, D, F), wmap),
                  pl.BlockSpec((None, F, D), wmap)],
        out_specs=pl.BlockSpec((tm, D), lambda i, te, used: (i, 0)),
        scratch_shapes=[pltpu.VMEM((D, F), BF16), pltpu.VMEM((D, F), BF16), pltpu.VMEM((F, D), BF16)],
    )
    return pl.pallas_call(
        _experts_kernel,
        grid_spec=grid_spec,
        out_shape=jax.ShapeDtypeStruct((P, D), F32),
        compiler_params=_params(("arbitrary",)),
        name="moe_experts",
    )(tile_expert, n_used, xs, w1, w3, w2)


def _ple_kernel(rec_ref, rec_next_ref, ys_ref, h_ref, route_ref, p_ref, gp_ref, wg_ref, bg_ref,
                wp_ref, gf_ref, o_ref, ybuf_a, ybuf_b, sems, *, tm):
    i = pl.program_id(0)
    half = tm // 2
    bufs = (ybuf_a, ybuf_b)

    def row_copies(rec_ref_, s, j, base):
        return (_row_copy(ys_ref, rec_ref_[ROW1, j], bufs[s].at[0], j - base, sems.at[s]),
                _row_copy(ys_ref, rec_ref_[ROW2, j], bufs[s].at[1], j - base, sems.at[s]))

    def drain(s):
        def wait(j, carry):
            _row_copy(ys_ref, 0, bufs[s].at[0], 0, sems.at[s]).wait()
            return carry

        lax.fori_loop(0, 2 * half, wait, 0, unroll=8)

    n_chunks = 4
    width = h_ref.shape[1] // n_chunks

    def compute(s, rows, requests):
        route = route_ref[rows, :]
        h = h_ref[rows, :] + route[:, R_W1:R_W1 + 1] * bufs[s][0] + route[:, R_W2:R_W2 + 1] * bufs[s][1]
        u = _rms(h, gp_ref[...]).astype(BF16)
        p = p_ref[rows, :].astype(BF16)
        per = len(requests) // n_chunks
        parts = []
        for c in range(n_chunks):
            for request in requests[c * per:(c + 1) * per]:
                for copy in row_copies(*request):
                    copy.start()
            cols = slice(c * width, (c + 1) * width)
            gate = jax.nn.sigmoid(jnp.dot(u, wg_ref[:, cols], preferred_element_type=F32) + bg_ref[:, cols])
            proj = jnp.dot(p, wp_ref[:, cols], preferred_element_type=F32)
            parts.append(h[:, cols] + gate * proj)
        o_ref[rows, :] = _rms(jnp.concatenate(parts, axis=-1), gf_ref[...])

    @pl.when(i == 0)
    def _():
        def issue(j, carry):
            for copy in row_copies(rec_ref, 0, j, 0):
                copy.start()
            return carry

        lax.fori_loop(0, half, issue, 0, unroll=8)

    drain(0)
    compute(0, slice(0, half), [(rec_ref, 1, j, half) for j in range(half, tm)])
    drain(1)
    compute(1, slice(half, tm), [(rec_next_ref, 0, j, 0) for j in range(half)])

    @pl.when(i + 1 == pl.num_programs(0))
    def _():
        drain(0)


def _ple(rec, ys, h, route, p2d, g_ple, w_gate, b_gate, w_proj, g_final, tm):
    T, D = h.shape
    P = p2d.shape[1]
    row = lambda i: (i, 0)
    n = T // tm
    return pl.pallas_call(
        functools.partial(_ple_kernel, tm=tm),
        grid=(n,),
        in_specs=[pl.BlockSpec((SUBLANES, tm), lambda i: (0, i), memory_space=pltpu.SMEM),
                  pl.BlockSpec((SUBLANES, tm), lambda i: (0, jnp.minimum(i + 1, n - 1)),
                               memory_space=pltpu.SMEM),
                  pl.BlockSpec(memory_space=pl.ANY),
                  pl.BlockSpec((tm, D), row), pl.BlockSpec((tm, LANES), row), pl.BlockSpec((tm, P), row),
                  _resident((1, D)), _resident(w_gate.shape), _resident((1, D)), _resident(w_proj.shape),
                  _resident((1, D))],
        out_specs=pl.BlockSpec((tm, D), row),
        out_shape=jax.ShapeDtypeStruct((T, D), F32),
        scratch_shapes=[pltpu.VMEM((2, tm // 2, D), F32), pltpu.VMEM((2, tm // 2, D), F32),
                        pltpu.SemaphoreType.DMA((2,))],
        compiler_params=_params(("arbitrary",)),
        name="ple_final",
    )(rec, rec, ys, h, route, p2d, g_ple, w_gate, b_gate, w_proj, g_final)


def _rope_tables(seq):
    half = ROPE_DIM // 2
    inv = ROPE_THETA ** (-(jnp.arange(half, dtype=F32) * 2.0) / ROPE_DIM)
    ang = jnp.arange(seq, dtype=F32)[:, None] * inv[None, :]
    cos, sin = jnp.cos(ang), jnp.sin(ang)
    zeros = jnp.zeros((seq, LANES - ROPE_DIM), F32)
    z16 = jnp.zeros((seq, half), F32)
    cos_t = jnp.concatenate([cos, cos, jnp.ones((seq, LANES - ROPE_DIM), F32)], axis=-1)
    sa_t = jnp.concatenate([z16, sin, zeros], axis=-1)
    sb_t = jnp.concatenate([-sin, z16, zeros], axis=-1)
    return cos_t, sa_t, sb_t


MOE_TILE = 256
OPROJ_TILE = 512
PLE_TILE = 512
DISPATCH_TILE = 1024


def _mixer(x, wts, cnt_in):
    batch, seq, D = x.shape
    T = batch * seq
    x2d = x.reshape(T, D)
    cos, sa, sb = _rope_tables(seq)
    qa, ka, va, qb, kb, vb, *strided = _qkv(x2d, wts["g_attn"], wts["w_in"], cos, sa, sb, seq, tm=256)
    n_dil = len(STRIDED_DILATIONS)
    qkv_by_dil = {1: (qa, ka, va)}
    for n, d in enumerate(STRIDED_DILATIONS):
        qkv_by_dil[d] = tuple(strided[i * n_dil + n] for i in range(3))
    o_list, lse_list = [], []
    for window, dil in DILATED_CONFIGS:
        o, lse = _attn_a(*qkv_by_dil[dil], batch, seq, dil, window // (2 * dil))
        o_list.append(o)
        lse_list.append(lse)
    mixed_b = _attn_b(qb, kb, vb, wts["sink"], wts["g_out_b"], batch, seq)
    return _oproj(o_list, lse_list, mixed_b, x2d, wts["w_o"], wts["g_out_a"], wts["g_ffn"],
                  wts["w_route"], wts["b_route"], cnt_in, tm=OPROJ_TILE, sub=OPROJ_TILE // 2)


def _sorted_layout(counts, n_tiles, tm):
    cnt = counts[0, :N_EXPERTS].astype(jnp.int32)
    tiles = (cnt + tm - 1) // tm
    ends = jnp.cumsum(tiles)
    offsets = (ends - tiles) * tm
    tile_expert = jnp.sum(jnp.arange(n_tiles, dtype=jnp.int32)[:, None] >= ends[None, :], axis=1)
    tile_expert = jnp.minimum(tile_expert, N_EXPERTS - 1).astype(jnp.int32)
    n_used = ends[-1]
    last_tile_start = jnp.where(tiles > 0, (ends - 1) * tm, -1)
    tail = n_used + jnp.arange(N_EXPERTS, dtype=jnp.int32)
    tail_start = jnp.where(tail < n_tiles, tail * tm, -1)
    fill_starts = jnp.concatenate([last_tile_start, tail_start]).astype(jnp.int32)
    return offsets, tile_expert, n_used.reshape(1).astype(jnp.int32), fill_starts


def kernel(x_prompt, x_sample, p_prompt, p_sample, g_attn, w_in, g_out_a, g_out_b, sink, w_o, g_ffn, w_group, b_group, w_router, b_router, w1, w3, w2, g_ple, w_ple_gate, b_ple_gate, w_ple_proj, g_final):
    depth = w_in.shape[0]
    assert depth == 1
    l = 0
    D = w_in.shape[1]
    pad = LANES - N_EXPERTS - N_GROUPS
    w_route = jnp.pad(jnp.concatenate([w_router[l], w_group[l]], axis=-1), ((0, 0), (0, pad)))
    w_route_hi = w_route.astype(BF16)
    w_route_lo = (w_route - w_route_hi.astype(F32)).astype(BF16)
    wts = {
        "g_attn": g_attn[l].reshape(1, D),
        "w_in": w_in[l].astype(BF16),
        "g_out_a": g_out_a[l].reshape(1, A_WIDTH),
        "g_out_b": g_out_b[l].reshape(1, B_WIDTH),
        "sink": sink[l],
        "w_o": w_o[l].astype(BF16),
        "g_ffn": g_ffn[l].reshape(1, D),
        "w_route": jnp.concatenate([w_route_hi, w_route_lo], axis=-1),
        "b_route": jnp.pad(jnp.concatenate([b_router[l], b_group[l]]), (0, pad)).reshape(1, LANES),
        "w1": w1[l],
        "w3": w3[l],
        "w2": w2[l],
        "g_ple": g_ple[l].reshape(1, D),
        "w_ple_gate": w_ple_gate[l].astype(BF16),
        "b_ple_gate": b_ple_gate[l].reshape(1, D),
        "w_ple_proj": w_ple_proj[l].astype(BF16),
        "g_final": g_final.reshape(1, D),
    }
    xs_list = (x_prompt, x_sample)
    ps_list = (p_prompt[l], p_sample[l])

    counts = jnp.zeros((1, LANES), F32)
    mixed = []
    route_ts = []
    for x in xs_list:
        h1, u2, route, route_t, counts = _mixer(x, wts, counts)
        mixed.append((h1, u2, route))
        route_ts.append(route_t)

    n_pairs = 2 * sum(x.shape[0] * x.shape[1] for x in xs_list)
    n_tiles = n_pairs // MOE_TILE + N_EXPERTS
    offsets, tile_expert, n_used, fill_starts = _sorted_layout(counts, n_tiles, MOE_TILE)
    recs = _position_records(route_ts, offsets)
    xs = _dispatch([u2 for _, u2, _ in mixed], recs, fill_starts, n_tiles * MOE_TILE,
                   DISPATCH_TILE, MOE_TILE)
    ys = _experts(xs, tile_expert, n_used, wts["w1"], wts["w3"], wts["w2"], MOE_TILE)

    outs = []
    for x, p, (h1, u2, route), rec in zip(xs_list, ps_list, mixed, recs):
        y = _ple(rec, ys, h1, route, p.reshape(h1.shape[0], -1),
                 wts["g_ple"], wts["w_ple_gate"], wts["b_ple_gate"], wts["w_ple_proj"], wts["g_final"],
                 PLE_TILE)
        outs.append(y.reshape(x.shape))
    return tuple(outs)
```

```python
import functools

import jax
import jax.numpy as jnp
import numpy as np
from jax import lax
from jax.experimental import pallas as pl
from jax.experimental.pallas import tpu as pltpu

HEAD_DIM = 128
A_HEADS = 8
A_WIDTH = A_HEADS * HEAD_DIM
B_HEADS = 8
B_KV_HEADS = 2
B_GROUP = B_HEADS // B_KV_HEADS
B_WIDTH = B_HEADS * HEAD_DIM
B_KV_WIDTH = B_KV_HEADS * HEAD_DIM
DILATED_CONFIGS = ((128, 1), (512, 4), (2048, 16))
B_RADIUS = 128
ROPE_THETA = 500000.0
ROPE_DIM = HEAD_DIM // 4
N_GROUPS = 4
EXPERTS_PER_GROUP = 8
N_EXPERTS = N_GROUPS * EXPERTS_PER_GROUP
RMS_EPS = 1e-6
NEG_INF = -1e30

LANES = 128
SUBLANES = 8
VMEM_LIMIT = 56 * 1024 * 1024

F32 = jnp.float32
BF16 = jnp.bfloat16


def _resident(shape):
    nd = len(shape)
    return pl.BlockSpec(shape, lambda *_: (0,) * nd, pipeline_mode=pl.Buffered(1))


def _params(sem):
    return pltpu.CompilerParams(dimension_semantics=sem, vmem_limit_bytes=VMEM_LIMIT)


def _rms(x, g):
    return x * lax.rsqrt(jnp.mean(x * x, axis=-1, keepdims=True) + RMS_EPS) * g


STRIDED_DILATIONS = tuple(d for _, d in DILATED_CONFIGS if d > 1)


def _qkv_kernel(x_ref, g_ref, w_ref, cos_ref, sa_ref, sb_ref, *rest):
    n_dil = len(STRIDED_DILATIONS)
    perm_refs = rest[:n_dil]
    qa_ref, ka_ref, va_ref, qb_ref, kb_ref, vb_ref = rest[n_dil:n_dil + 6]
    dil_refs = [rest[n_dil + 6 + i * n_dil:n_dil + 6 + (i + 1) * n_dil] for i in range(3)]
    tm = x_ref.shape[0]
    u = _rms(x_ref[...], g_ref[...]).astype(BF16)
    cos = cos_ref[...]
    sa = sa_ref[...]
    sb = sb_ref[...]
    half = ROPE_DIM // 2
    scale = HEAD_DIM ** -0.5

    def rope(t):
        return t * cos + pltpu.roll(t, half, 1) * sa + pltpu.roll(t, LANES - half, 1) * sb

    sections = (
        (qa_ref, 0, A_HEADS, True, scale, 0),
        (ka_ref, A_WIDTH, A_HEADS, True, None, 1),
        (va_ref, 2 * A_WIDTH, A_HEADS, False, None, 2),
        (qb_ref, 3 * A_WIDTH, B_HEADS, True, scale, None),
        (kb_ref, 3 * A_WIDTH + B_WIDTH, B_KV_HEADS, True, None, None),
        (vb_ref, 3 * A_WIDTH + B_WIDTH + B_KV_WIDTH, B_KV_HEADS, False, None, None),
    )
    for out_ref, c0, heads, rotary, sc, a_idx in sections:
        for h0 in range(0, heads, 4):
            nh = min(4, heads - h0)
            lo = c0 + h0 * HEAD_DIM
            acc = jnp.dot(u, w_ref[:, lo:lo + nh * HEAD_DIM], preferred_element_type=F32)
            for j in range(nh):
                t = acc[:, j * HEAD_DIM:(j + 1) * HEAD_DIM]
                if rotary:
                    t = rope(t)
                if sc is not None:
                    t = t * sc
                cols = slice((h0 + j) * HEAD_DIM, (h0 + j + 1) * HEAD_DIM)
                out_ref[:, cols] = t.astype(BF16)
        if a_idx is not None:
            for d, perm_ref, dref in zip(STRIDED_DILATIONS, perm_refs, dil_refs[a_idx]):
                n = tm // d
                permuted = jnp.dot(perm_ref[...], out_ref[...], preferred_element_type=F32).astype(BF16)
                for r in range(d):
                    dref[:, r * A_WIDTH:(r + 1) * A_WIDTH] = permuted[r * n:(r + 1) * n, :]


def _qkv(x2d, g, w_in, cos, sa, sb, seq, tm):
    T, D = x2d.shape
    n_pos = seq // tm
    row = lambda i: (i, 0)
    tab = pl.BlockSpec((tm, LANES), lambda i: (i % n_pos, 0))
    widths = (A_WIDTH, A_WIDTH, A_WIDTH, B_WIDTH, B_KV_WIDTH, B_KV_WIDTH)
    out_specs = [pl.BlockSpec((tm, w), row) for w in widths]
    out_shape = [jax.ShapeDtypeStruct((T, w), BF16) for w in widths]
    for _ in range(3):
        for d in STRIDED_DILATIONS:
            out_specs.append(pl.BlockSpec((tm // d, d * A_WIDTH), row))
            out_shape.append(jax.ShapeDtypeStruct((T // d, d * A_WIDTH), BF16))
    perms = [_residue_permutation(tm, d) for d in STRIDED_DILATIONS]
    return pl.pallas_call(
        _qkv_kernel,
        grid=(T // tm,),
        in_specs=[pl.BlockSpec((tm, D), row), _resident((1, D)), _resident(w_in.shape), tab, tab, tab,
                  *[_resident((tm, tm)) for _ in perms]],
        out_specs=out_specs,
        out_shape=out_shape,
        compiler_params=_params(("parallel",)),
        name="qkv",
    )(x2d, g, w_in, cos, sa, sb, *perms)


def _residue_permutation(n, d):
    dst = np.arange(n)
    src = (dst % (n // d)) * d + dst // (n // d)
    p = np.zeros((n, n), np.float32)
    p[dst, src] = 1.0
    return jnp.asarray(p, BF16)


ATTN_SUB = 128
ATTN_BLOCK = 1024


def _band_mask(first_q, sub, radius, length):
    shape = (sub, sub + 2 * radius)
    row = lax.broadcasted_iota(jnp.int32, shape, 0)
    col = lax.broadcasted_iota(jnp.int32, shape, 1)
    kpos = first_q - radius + col
    return (jnp.abs(col - radius - row) <= radius) & (kpos >= 0) & (kpos < length)


def _fill_window(ext_ref, prev_ref, cur_ref, next_ref, radius):
    n = cur_ref.shape[0]
    ext_ref[0:radius, :] = prev_ref[...]
    ext_ref[radius:radius + n, :] = cur_ref[...]
    ext_ref[radius + n:, :] = next_ref[...]


_NT = (((1,), (1,)), ((), ()))


def _attn_a_kernel(q_ref, kp_ref, kc_ref, kn_ref, vp_ref, vc_ref, vn_ref, o_ref, lse_ref, kx_ref, vx_ref,
                   *, length, lq, radius, n_res):
    sub = min(ATTN_SUB, lq)
    _fill_window(kx_ref, kp_ref, kc_ref, kn_ref, radius)
    _fill_window(vx_ref, vp_ref, vc_ref, vn_ref, radius)
    lane = lax.broadcasted_iota(jnp.int32, (sub, LANES), 1)
    for j in range(lq // sub):
        valid = _band_mask(pl.program_id(2) * lq + j * sub, sub, radius, length)
        q_rows = slice(j * sub, (j + 1) * sub)
        k_rows = slice(j * sub, (j + 1) * sub + 2 * radius)
        for res in range(n_res):
            cols = [slice(res * A_WIDTH + h * HEAD_DIM, res * A_WIDTH + (h + 1) * HEAD_DIM)
                    for h in range(A_HEADS)]
            q = jnp.stack([q_ref[q_rows, sl] for sl in cols], axis=0)
            k = jnp.stack([kx_ref[k_rows, sl] for sl in cols], axis=0)
            v = jnp.stack([vx_ref[k_rows, sl] for sl in cols], axis=0)
            s = jnp.einsum("hqd,hkd->hqk", q, k, preferred_element_type=F32)
            s = jnp.where(valid[None], s, NEG_INF)
            m = jnp.max(s, axis=-1, keepdims=True)
            p = jnp.exp(s - m)
            l = jnp.sum(p, axis=-1, keepdims=True)
            o = jnp.einsum("hqk,hkd->hqd", p.astype(BF16), v, preferred_element_type=F32) / l
            lse = m + jnp.log(l)
            lse_all = jnp.zeros((sub, LANES), F32)
            for h, sl in enumerate(cols):
                o_ref[q_rows, sl] = o[h].astype(o_ref.dtype)
                lse_all = jnp.where(lane == h, lse[h], lse_all)
            lse_ref[q_rows, res * LANES:(res + 1) * LANES] = lse_all


def _attn_a(qa, ka, va, batch, seq, dil, radius):
    ls = seq // dil
    lq = min(ATTN_BLOCK, ls)
    n_res = min(dil, ATTN_BLOCK // lq)
    per = lq // radius
    n_halo = ls // radius
    view = lambda t: t.reshape(batch, ls, dil * A_WIDTH)
    w = n_res * A_WIDTH
    cur = pl.BlockSpec((None, lq, w), lambda b, r, i: (b, i, r))
    prv = pl.BlockSpec((None, radius, w), lambda b, r, i: (b, jnp.maximum(i * per - 1, 0), r))
    nxt = pl.BlockSpec((None, radius, w), lambda b, r, i: (b, jnp.minimum((i + 1) * per, n_halo - 1), r))
    o, lse = pl.pallas_call(
        functools.partial(_attn_a_kernel, length=ls, lq=lq, radius=radius, n_res=n_res),
        grid=(batch, dil // n_res, ls // lq),
        in_specs=[cur, prv, cur, nxt, prv, cur, nxt],
        out_specs=[cur, pl.BlockSpec((None, lq, n_res * LANES), lambda b, r, i: (b, i, r))],
        out_shape=[jax.ShapeDtypeStruct((batch, ls, dil * A_WIDTH), BF16),
                   jax.ShapeDtypeStruct((batch, ls, dil * LANES), F32)],
        scratch_shapes=[pltpu.VMEM((lq + 2 * radius, w), BF16), pltpu.VMEM((lq + 2 * radius, w), BF16)],
        compiler_params=_params(("parallel", "parallel", "parallel")),
        name=f"attn_a_d{dil}",
    )(view(qa), view(ka), view(ka), view(ka), view(va), view(va), view(va))
    return o.reshape(batch * ls, dil * A_WIDTH), lse.reshape(batch * ls, dil * LANES)


def _attn_b_kernel(sink_ref, q_ref, kp_ref, kc_ref, kn_ref, vp_ref, vc_ref, vn_ref, g_ref, o_ref,
                   kx_ref, vx_ref, *, length, lq, radius):
    sub = ATTN_SUB
    _fill_window(kx_ref, kp_ref, kc_ref, kn_ref, radius)
    _fill_window(vx_ref, vp_ref, vc_ref, vn_ref, radius)
    for j in range(lq // sub):
        valid = _band_mask(pl.program_id(1) * lq + j * sub, sub, radius, length)
        q_rows = slice(j * sub, (j + 1) * sub)
        k_rows = slice(j * sub, (j + 1) * sub + 2 * radius)
        outs = []
        for g in range(B_KV_HEADS):
            kv = slice(g * HEAD_DIM, (g + 1) * HEAD_DIM)
            heads = range(g * B_GROUP, (g + 1) * B_GROUP)
            q = jnp.concatenate([q_ref[q_rows, h * HEAD_DIM:(h + 1) * HEAD_DIM] for h in heads], axis=0)
            s = lax.dot_general(q, kx_ref[k_rows, kv], _NT, preferred_element_type=F32)
            s = jnp.where(valid[None], s.reshape(B_GROUP, sub, sub + 2 * radius), NEG_INF)
            head = lax.broadcasted_iota(jnp.int32, (B_GROUP, 1, 1), 0)
            sink = jnp.zeros((B_GROUP, 1, 1), F32)
            for n, h in enumerate(heads):
                sink = jnp.where(head == n, sink_ref[h], sink)
            m = jnp.maximum(jnp.max(s, axis=-1, keepdims=True), sink)
            p = jnp.exp(s - m)
            l = jnp.sum(p, axis=-1, keepdims=True) + jnp.exp(sink - m)
            o = jnp.dot(p.reshape(B_GROUP * sub, sub + 2 * radius).astype(BF16), vx_ref[k_rows, kv],
                        preferred_element_type=F32).reshape(B_GROUP, sub, HEAD_DIM) / l
            outs.extend(o[n] for n in range(B_GROUP))
        o_ref[q_rows, :] = _rms(jnp.concatenate(outs, axis=-1), g_ref[...]).astype(o_ref.dtype)


def _attn_b(qb, kb, vb, sink, g_out_b, batch, seq):
    radius = B_RADIUS
    lq = min(ATTN_BLOCK, seq)
    per = lq // radius
    n_halo = seq // radius
    q3 = qb.reshape(batch, seq, B_WIDTH)
    k3 = kb.reshape(batch, seq, B_KV_WIDTH)
    v3 = vb.reshape(batch, seq, B_KV_WIDTH)
    qspec = pl.BlockSpec((None, lq, B_WIDTH), lambda b, i: (b, i, 0))
    cur = pl.BlockSpec((None, lq, B_KV_WIDTH), lambda b, i: (b, i, 0))
    prv = pl.BlockSpec((None, radius, B_KV_WIDTH), lambda b, i: (b, jnp.maximum(i * per - 1, 0), 0))
    nxt = pl.BlockSpec((None, radius, B_KV_WIDTH),
                       lambda b, i: (b, jnp.minimum((i + 1) * per, n_halo - 1), 0))
    ext = pltpu.VMEM((lq + 2 * radius, B_KV_WIDTH), BF16)
    out = pl.pallas_call(
        functools.partial(_attn_b_kernel, length=seq, lq=lq, radius=radius),
        grid=(batch, seq // lq),
        in_specs=[pl.BlockSpec(memory_space=pltpu.SMEM), qspec, prv, cur, nxt, prv, cur, nxt,
                  _resident((1, B_WIDTH))],
        out_specs=qspec,
        out_shape=jax.ShapeDtypeStruct((batch, seq, B_WIDTH), BF16),
        scratch_shapes=[ext, ext],
        compiler_params=_params(("parallel", "parallel")),
        name="attn_b",
    )(sink, q3, k3, k3, k3, v3, v3, v3, g_out_b)
    return out.reshape(batch * seq, B_WIDTH)


R_E1, R_E2, R_W1, R_W2, R_RANK1, R_RANK2 = range(6)


def _oproj_kernel(o1_ref, o4_ref, o16_ref, l1_ref, l4_ref, l16_ref, mb_ref, x_ref, wo_ref,
                  ga_ref, gf_ref, wr_ref, br_ref, cnt_in_ref, h_ref, u_ref, route_ref, route_t_ref, cnt_ref,
                  o4n_ref, o16n_ref, l4n_ref, l16n_ref, *, sub):
    tm = x_ref.shape[0]

    @pl.when(pl.program_id(0) == 0)
    def _():
        cnt_ref[...] = cnt_in_ref[...]

    for d, o_src, o_dst, l_src, l_dst in ((STRIDED_DILATIONS[0], o4_ref, o4n_ref, l4_ref, l4n_ref),
                                          (STRIDED_DILATIONS[1], o16_ref, o16n_ref, l16_ref, l16n_ref)):
        for r in range(d):
            rows = pl.ds(r, tm // d, stride=d)
            for h in range(A_HEADS):
                lo = r * A_WIDTH + h * HEAD_DIM
                o_dst[h, rows, :] = o_src[:, lo:lo + HEAD_DIM].astype(F32)
            l_dst[rows, :] = l_src[:, r * LANES:(r + 1) * LANES]

    for c in range(tm // sub):
        _oproj_rows(slice(c * sub, (c + 1) * sub), o1_ref, o4n_ref, o16n_ref, l1_ref, l4n_ref, l16n_ref,
                    mb_ref, x_ref, wo_ref, ga_ref, gf_ref, wr_ref, br_ref, h_ref, u_ref, route_ref,
                    route_t_ref, cnt_ref)


def _oproj_rows(rs, o1_ref, o2_ref, o3_ref, l1_ref, l2_ref, l3_ref, mb_ref, x_ref, wo_ref,
                ga_ref, gf_ref, wr_ref, br_ref, h_ref, u_ref, route_ref, route_t_ref, cnt_ref):
    tm = rs.stop - rs.start
    l1, l2, l3 = l1_ref[rs, :], l2_ref[rs, :], l3_ref[rs, :]
    mx = jnp.maximum(jnp.maximum(l1, l2), l3)
    e1, e2, e3 = jnp.exp(l1 - mx), jnp.exp(l2 - mx), jnp.exp(l3 - mx)
    inv = 1.0 / (e1 + e2 + e3)
    w1, w2, w3 = e1 * inv, e2 * inv, e3 * inv
    parts = []
    for h in range(A_HEADS):
        sl = slice(h * HEAD_DIM, (h + 1) * HEAD_DIM)
        bc = lambda w: jnp.broadcast_to(w[:, h:h + 1], (tm, HEAD_DIM))
        parts.append(bc(w1) * o1_ref[rs, sl].astype(F32) + bc(w2) * o2_ref[h, rs, :]
                     + bc(w3) * o3_ref[h, rs, :])
    mixed_a = _rms(jnp.concatenate(parts, axis=-1), ga_ref[...]).astype(BF16)
    h = (x_ref[rs, :]
         + jnp.dot(mixed_a, wo_ref[:A_WIDTH, :], preferred_element_type=F32)
         + jnp.dot(mb_ref[rs, :], wo_ref[A_WIDTH:, :], preferred_element_type=F32))
    h_ref[rs, :] = h
    u = _rms(h, gf_ref[...])
    u_ref[rs, :] = u

    u_hi = u.astype(BF16)
    u_lo = (u - u_hi.astype(F32)).astype(BF16)
    z = (jnp.dot(u_hi, wr_ref[...], preferred_element_type=F32)
         + jnp.dot(u_lo, wr_ref[...], preferred_element_type=F32))
    logits = z[:, :LANES] + z[:, LANES:] + br_ref[...]

    lt = logits.T
    el = lt[:N_EXPERTS, :]
    gl = lt[N_EXPERTS:N_EXPERTS + SUBLANES, :]
    g_row = lax.broadcasted_iota(jnp.int32, gl.shape, 0)
    e_row = lax.broadcasted_iota(jnp.int32, el.shape, 0)
    big = jnp.int32(LANES)
    is_group = g_row < N_GROUPS
    gmax = jnp.max(jnp.where(is_group, gl, -jnp.inf), axis=0, keepdims=True)
    gidx = jnp.min(jnp.where(is_group & (gl == gmax), g_row, big), axis=0, keepdims=True)
    ggate = 1.0 / jnp.sum(jnp.where(is_group, jnp.exp(gl - gmax), 0.0), axis=0, keepdims=True)
    in_group = (e_row >= gidx * EXPERTS_PER_GROUP) & (e_row < (gidx + 1) * EXPERTS_PER_GROUP)
    v1 = jnp.max(jnp.where(in_group, el, -jnp.inf), axis=0, keepdims=True)
    i1 = jnp.min(jnp.where(in_group & (el == v1), e_row, big), axis=0, keepdims=True)
    rest = in_group & (e_row != i1)
    v2 = jnp.max(jnp.where(rest, el, -jnp.inf), axis=0, keepdims=True)
    i2 = jnp.min(jnp.where(rest & (el == v2), e_row, big), axis=0, keepdims=True)
    t = jnp.exp(v2 - v1)
    tw1 = ggate / (1.0 + t)
    tw2 = ggate * t / (1.0 + t)

    onehot = ((e_row == i1) | (e_row == i2)).astype(F32)
    c_i = lax.broadcasted_iota(jnp.int32, (tm, tm), 0)
    r_i = lax.broadcasted_iota(jnp.int32, (tm, tm), 1)
    before = (c_i < r_i).astype(BF16)
    diag = (lax.broadcasted_iota(jnp.int32, (N_EXPERTS, LANES), 0)
            == lax.broadcasted_iota(jnp.int32, (N_EXPERTS, LANES), 1))
    cnt_col = jnp.sum(jnp.where(diag, cnt_ref[...], 0.0), axis=1, keepdims=True)
    prefix = jnp.dot(onehot.astype(BF16), before, preferred_element_type=F32) + cnt_col
    rank1 = jnp.sum(jnp.where(e_row == i1, prefix, 0.0), axis=0, keepdims=True)
    rank2 = jnp.sum(jnp.where(e_row == i2, prefix, 0.0), axis=0, keepdims=True)
    added = jnp.sum(onehot, axis=1, keepdims=True)
    cnt_ref[...] += jnp.sum(jnp.where(diag, added, 0.0), axis=0, keepdims=True)

    rec_row = lax.broadcasted_iota(jnp.int32, (LANES, tm), 0)
    rec_t = jnp.zeros((LANES, tm), F32)
    for slot, val in ((R_E1, i1.astype(F32)), (R_E2, i2.astype(F32)), (R_W1, tw1), (R_W2, tw2),
                      (R_RANK1, rank1), (R_RANK2, rank2)):
        rec_t = jnp.where(rec_row == slot, val, rec_t)
    route_t_ref[:, rs] = rec_t[:SUBLANES, :]
    route_ref[rs, :] = rec_t.T


def _oproj(o_list, lse_list, mixed_b, x2d, w_o, g_out_a, g_ffn, w_route, b_route, cnt_in, tm, sub):
    T, D = x2d.shape
    row = lambda i: (i, 0)
    dils = (1,) + STRIDED_DILATIONS
    aspecs = [pl.BlockSpec((tm // d, d * A_WIDTH), row) for d in dils]
    lspecs = [pl.BlockSpec((tm // d, d * LANES), row) for d in dils]
    lspec = pl.BlockSpec((tm, LANES), row)
    dspec = pl.BlockSpec((tm, D), row)
    cspec = pl.BlockSpec((1, LANES), lambda i: (0, 0))
    return pl.pallas_call(
        functools.partial(_oproj_kernel, sub=sub),
        grid=(T // tm,),
        in_specs=[*aspecs, *lspecs, pl.BlockSpec((tm, B_WIDTH), row), dspec,
                  _resident(w_o.shape), _resident((1, A_WIDTH)), _resident((1, D)),
                  _resident(w_route.shape), _resident((1, LANES)), cspec],
        out_specs=[dspec, dspec, lspec, pl.BlockSpec((SUBLANES, tm), lambda i: (0, i)), cspec],
        out_shape=[jax.ShapeDtypeStruct((T, D), F32), jax.ShapeDtypeStruct((T, D), F32),
                   jax.ShapeDtypeStruct((T, LANES), F32), jax.ShapeDtypeStruct((SUBLANES, T), F32),
                   jax.ShapeDtypeStruct((1, LANES), F32)],
        scratch_shapes=[pltpu.VMEM((A_HEADS, tm, HEAD_DIM), F32), pltpu.VMEM((A_HEADS, tm, HEAD_DIM), F32),
                        pltpu.VMEM((tm, LANES), F32), pltpu.VMEM((tm, LANES), F32)],
        compiler_params=_params(("arbitrary",)),
        name="oproj_route",
    )(*o_list, *lse_list, mixed_b, x2d, w_o, g_out_a, g_ffn, w_route, b_route, cnt_in)


def _row_copy(src_ref, src_row, dst_ref, dst_row, sem):
    return pltpu.make_async_copy(src_ref.at[pl.ds(src_row, 1)], dst_ref.at[pl.ds(dst_row, 1)], sem)


ROW1, ROW2 = R_E1, R_E2


def _position_records(route_ts, offsets):
    rec = jnp.concatenate(route_ts, axis=1).astype(jnp.int32)
    expert = jnp.stack([rec[R_E1], rec[R_E2]])
    rank = jnp.stack([rec[R_RANK1], rec[R_RANK2]])
    seg = jnp.zeros_like(expert)
    for e in range(N_EXPERTS):
        seg = jnp.where(expert == e, offsets[e], seg)
    rows = jnp.concatenate([seg + rank, jnp.zeros((SUBLANES - 2, rec.shape[1]), jnp.int32)], axis=0)
    sizes = np.cumsum([r.shape[1] for r in route_ts])[:-1]
    return jnp.split(rows, sizes, axis=1)


def _dispatch_kernel(fill_ref, *rest, tm, tiles_per_src, fill_rows):
    n_src = len(tiles_per_src)
    rec_refs = rest[:n_src]
    u_refs = rest[n_src:2 * n_src]
    xs_ref, sem, zero_ref, fill_sem = rest[2 * n_src:]
    i = pl.program_id(0)

    @pl.when(i == 0)
    def _():
        zero_ref[...] = jnp.zeros_like(zero_ref)

        def tile_copy(k):
            start = pl.multiple_of(fill_ref[k], fill_rows)
            return pltpu.make_async_copy(zero_ref, xs_ref.at[pl.ds(start, fill_rows)], fill_sem)

        for k in range(fill_ref.shape[0]):
            @pl.when(fill_ref[k] >= 0)
            def _():
                tile_copy(k).start()

        for k in range(fill_ref.shape[0]):
            @pl.when(fill_ref[k] >= 0)
            def _():
                tile_copy(k).wait()

    def scatter(rec_ref, u_ref):
        def issue(j, carry):
            for row in (rec_ref[ROW1, j], rec_ref[ROW2, j]):
                _row_copy(u_ref, j, xs_ref, row, sem).start()
            return carry

        lax.fori_loop(0, tm, issue, 0, unroll=8)

        def drain(j, carry):
            _row_copy(u_ref, 0, xs_ref, 0, sem).wait()
            return carry

        lax.fori_loop(0, 2 * tm, drain, 0, unroll=8)

    first = 0
    for rec_ref, u_ref, n in zip(rec_refs, u_refs, tiles_per_src):
        @pl.when((i >= first) & (i < first + n))
        def _():
            scatter(rec_ref, u_ref)

        first += n


def _dispatch(us, recs, fill_starts, n_rows, tm, fill_rows):
    D = us[0].shape[1]
    tiles_per_src = tuple(u.shape[0] // tm for u in us)
    in_specs = [pl.BlockSpec(memory_space=pltpu.SMEM)]
    starts = [sum(tiles_per_src[:k]) for k in range(len(us))]
    for first, n in zip(starts, tiles_per_src):
        in_specs.append(pl.BlockSpec((SUBLANES, tm), lambda i, first=first, n=n: (0, jnp.clip(i - first, 0, n - 1)),
                                     memory_space=pltpu.SMEM))
    for first, n in zip(starts, tiles_per_src):
        in_specs.append(pl.BlockSpec((tm, D), lambda i, first=first, n=n: (jnp.clip(i - first, 0, n - 1), 0)))
    first = sum(tiles_per_src)
    return pl.pallas_call(
        functools.partial(_dispatch_kernel, tm=tm, tiles_per_src=tiles_per_src, fill_rows=fill_rows),
        grid=(first,),
        in_specs=in_specs,
        out_specs=pl.BlockSpec(memory_space=pl.ANY),
        out_shape=jax.ShapeDtypeStruct((n_rows, D), F32),
        scratch_shapes=[pltpu.SemaphoreType.DMA(()), pltpu.VMEM((fill_rows, D), F32),
                        pltpu.SemaphoreType.DMA(())],
        compiler_params=_params(("arbitrary",)),
        name="moe_dispatch",
    )(fill_starts, *recs, *us)


def _experts_kernel(te_ref, used_ref, x_ref, w1_ref, w3_ref, w2_ref, y_ref, w1b_ref, w3b_ref, w2b_ref):
    i = pl.program_id(0)
    cur = jnp.minimum(i, used_ref[0] - 1)
    new_expert = (i == 0) | (te_ref[cur] != te_ref[jnp.maximum(cur - 1, 0)])

    @pl.when(new_expert & (i < used_ref[0]))
    def _():
        w1b_ref[...] = w1_ref[...].astype(BF16)
        w3b_ref[...] = w3_ref[...].astype(BF16)
        w2b_ref[...] = w2_ref[...].astype(BF16)

    @pl.when(i < used_ref[0])
    def _():
        x = x_ref[...].astype(BF16)
        a = jnp.dot(x, w1b_ref[...], preferred_element_type=F32)
        b = jnp.dot(x, w3b_ref[...], preferred_element_type=F32)
        hid = (a * jax.nn.sigmoid(a)) * b
        y_ref[...] = jnp.dot(hid.astype(BF16), w2b_ref[...], preferred_element_type=F32)

    @pl.when(i >= used_ref[0])
    def _():
        y_ref[...] = jnp.zeros_like(y_ref)


def _experts(xs, tile_expert, n_used, w1, w3, w2, tm):
    P, D = xs.shape
    F = w1.shape[2]
    row = lambda i, te, used: (jnp.minimum(i, used[0] - 1), 0)
    wmap = lambda i, te, used: (te[jnp.minimum(i, used[0] - 1)], 0, 0)
    grid_spec = pltpu.PrefetchScalarGridSpec(
        num_scalar_prefetch=2,
        grid=(P // tm,),
        in_specs=[pl.BlockSpec((tm, D), row),
                  pl.BlockSpec((None, D, F), wmap),
                  pl.BlockSpec((None, D, F), wmap),
                  pl.BlockSpec((None, F, D), wmap)],
        out_specs=pl.BlockSpec((tm, D), lambda i, te, used: (i, 0)),
        scratch_shapes=[pltpu.VMEM((D, F), BF16), pltpu.VMEM((D, F), BF16), pltpu.VMEM((F, D), BF16)],
    )
    return pl.pallas_call(
        _experts_kernel,
        grid_spec=grid_spec,
        out_shape=jax.ShapeDtypeStruct((P, D), F32),
        compiler_params=_params(("arbitrary",)),
        name="moe_experts",
    )(tile_expert, n_used, xs, w1, w3, w2)


def _ple_kernel(rec_ref, rec_next_ref, ys_ref, h_ref, route_ref, p_ref, gp_ref, wg_ref, bg_ref,
                wp_ref, gf_ref, o_ref, ybuf_a, ybuf_b, sems, *, tm):
    i = pl.program_id(0)
    half = tm // 2
    bufs = (ybuf_a, ybuf_b)

    def row_copies(rec_ref_, s, j, base):
        return (_row_copy(ys_ref, rec_ref_[ROW1, j], bufs[s].at[0], j - base, sems.at[s]),
                _row_copy(ys_ref, rec_ref_[ROW2, j], bufs[s].at[1], j - base, sems.at[s]))

    def drain(s):
        def wait(j, carry):
            _row_copy(ys_ref, 0, bufs[s].at[0], 0, sems.at[s]).wait()
            return carry

        lax.fori_loop(0, 2 * half, wait, 0, unroll=8)

    n_chunks = 4
    width = h_ref.shape[1] // n_chunks

    def compute(s, rows, requests):
        route = route_ref[rows, :]
        h = h_ref[rows, :] + route[:, R_W1:R_W1 + 1] * bufs[s][0] + route[:, R_W2:R_W2 + 1] * bufs[s][1]
        u = _rms(h, gp_ref[...]).astype(BF16)
        p = p_ref[rows, :].astype(BF16)
        per = len(requests) // n_chunks
        parts = []
        for c in range(n_chunks):
            for request in requests[c * per:(c + 1) * per]:
                for copy in row_copies(*request):
                    copy.start()
            cols = slice(c * width, (c + 1) * width)
            gate = jax.nn.sigmoid(jnp.dot(u, wg_ref[:, cols], preferred_element_type=F32) + bg_ref[:, cols])
            proj = jnp.dot(p, wp_ref[:, cols], preferred_element_type=F32)
            parts.append(h[:, cols] + gate * proj)
        o_ref[rows, :] = _rms(jnp.concatenate(parts, axis=-1), gf_ref[...])

    @pl.when(i == 0)
    def _():
        def issue(j, carry):
            for copy in row_copies(rec_ref, 0, j, 0):
                copy.start()
            return carry

        lax.fori_loop(0, half, issue, 0, unroll=8)

    drain(0)
    compute(0, slice(0, half), [(rec_ref, 1, j, half) for j in range(half, tm)])
    drain(1)
    compute(1, slice(half, tm), [(rec_next_ref, 0, j, 0) for j in range(half)])

    @pl.when(i + 1 == pl.num_programs(0))
    def _():
        drain(0)


def _ple(rec, ys, h, route, p2d, g_ple, w_gate, b_gate, w_proj, g_final, tm):
    T, D = h.shape
    P = p2d.shape[1]
    row = lambda i: (i, 0)
    n = T // tm
    return pl.pallas_call(
        functools.partial(_ple_kernel, tm=tm),
        grid=(n,),
        in_specs=[pl.BlockSpec((SUBLANES, tm), lambda i: (0, i), memory_space=pltpu.SMEM),
                  pl.BlockSpec((SUBLANES, tm), lambda i: (0, jnp.minimum(i + 1, n - 1)),
                               memory_space=pltpu.SMEM),
                  pl.BlockSpec(memory_space=pl.ANY),
                  pl.BlockSpec((tm, D), row), pl.BlockSpec((tm, LANES), row), pl.BlockSpec((tm, P), row),
                  _resident((1, D)), _resident(w_gate.shape), _resident((1, D)), _resident(w_proj.shape),
                  _resident((1, D))],
        out_specs=pl.BlockSpec((tm, D), row),
        out_shape=jax.ShapeDtypeStruct((T, D), F32),
        scratch_shapes=[pltpu.VMEM((2, tm // 2, D), F32), pltpu.VMEM((2, tm // 2, D), F32),
                        pltpu.SemaphoreType.DMA((2,))],
        compiler_params=_params(("arbitrary",)),
        name="ple_final",
    )(rec, rec, ys, h, route, p2d, g_ple, w_gate, b_gate, w_proj, g_final)


def _rope_tables(seq):
    half = ROPE_DIM // 2
    inv = ROPE_THETA ** (-(jnp.arange(half, dtype=F32) * 2.0) / ROPE_DIM)
    ang = jnp.arange(seq, dtype=F32)[:, None] * inv[None, :]
    cos, sin = jnp.cos(ang), jnp.sin(ang)
    zeros = jnp.zeros((seq, LANES - ROPE_DIM), F32)
    z16 = jnp.zeros((seq, half), F32)
    cos_t = jnp.concatenate([cos, cos, jnp.ones((seq, LANES - ROPE_DIM), F32)], axis=-1)
    sa_t = jnp.concatenate([z16, sin, zeros], axis=-1)
    sb_t = jnp.concatenate([-sin, z16, zeros], axis=-1)
    return cos_t, sa_t, sb_t


MOE_TILE = 256
OPROJ_TILE = 512
PLE_TILE = 512
DISPATCH_TILE = 1024


def _mixer(x, wts, cnt_in):
    batch, seq, D = x.shape
    T = batch * seq
    x2d = x.reshape(T, D)
    cos, sa, sb = _rope_tables(seq)
    qa, ka, va, qb, kb, vb, *strided = _qkv(x2d, wts["g_attn"], wts["w_in"], cos, sa, sb, seq, tm=256)
    n_dil = len(STRIDED_DILATIONS)
    qkv_by_dil = {1: (qa, ka, va)}
    for n, d in enumerate(STRIDED_DILATIONS):
        qkv_by_dil[d] = tuple(strided[i * n_dil + n] for i in range(3))
    o_list, lse_list = [], []
    for window, dil in DILATED_CONFIGS:
        o, lse = _attn_a(*qkv_by_dil[dil], batch, seq, dil, window // (2 * dil))
        o_list.append(o)
        lse_list.append(lse)
    mixed_b = _attn_b(qb, kb, vb, wts["sink"], wts["g_out_b"], batch, seq)
    return _oproj(o_list, lse_list, mixed_b, x2d, wts["w_o"], wts["g_out_a"], wts["g_ffn"],
                  wts["w_route"], wts["b_route"], cnt_in, tm=OPROJ_TILE, sub=OPROJ_TILE // 2)


def _sorted_layout(counts, n_tiles, tm):
    cnt = counts[0, :N_EXPERTS].astype(jnp.int32)
    tiles = (cnt + tm - 1) // tm
    ends = jnp.cumsum(tiles)
    offsets = (ends - tiles) * tm
    tile_expert = jnp.sum(jnp.arange(n_tiles, dtype=jnp.int32)[:, None] >= ends[None, :], axis=1)
    tile_expert = jnp.minimum(tile_expert, N_EXPERTS - 1).astype(jnp.int32)
    n_used = ends[-1]
    last_tile_start = jnp.where(tiles > 0, (ends - 1) * tm, -1)
    tail = n_used + jnp.arange(N_EXPERTS, dtype=jnp.int32)
    tail_start = jnp.where(tail < n_tiles, tail * tm, -1)
    fill_starts = jnp.concatenate([last_tile_start, tail_start]).astype(jnp.int32)
    return offsets, tile_expert, n_used.reshape(1).astype(jnp.int32), fill_starts


def kernel(x_prompt, x_sample, p_prompt, p_sample, g_attn, w_in, g_out_a, g_out_b, sink, w_o, g_ffn, w_group, b_group, w_router, b_router, w1, w3, w2, g_ple, w_ple_gate, b_ple_gate, w_ple_proj, g_final):
    depth = w_in.shape[0]
    assert depth == 1
    l = 0
    D = w_in.shape[1]
    pad = LANES - N_EXPERTS - N_GROUPS
    w_route = jnp.pad(jnp.concatenate([w_router[l], w_group[l]], axis=-1), ((0, 0), (0, pad)))
    w_route_hi = w_route.astype(BF16)
    w_route_lo = (w_route - w_route_hi.astype(F32)).astype(BF16)
    wts = {
        "g_attn": g_attn[l].reshape(1, D),
        "w_in": w_in[l].astype(BF16),
        "g_out_a": g_out_a[l].reshape(1, A_WIDTH),
        "g_out_b": g_out_b[l].reshape(1, B_WIDTH),
        "sink": sink[l],
        "w_o": w_o[l].astype(BF16),
        "g_ffn": g_ffn[l].reshape(1, D),
        "w_route": jnp.concatenate([w_route_hi, w_route_lo], axis=-1),
        "b_route": jnp.pad(jnp.concatenate([b_router[l], b_group[l]]), (0, pad)).reshape(1, LANES),
        "w1": w1[l],
        "w3": w3[l],
        "w2": w2[l],
        "g_ple": g_ple[l].reshape(1, D),
        "w_ple_gate": w_ple_gate[l].astype(BF16),
        "b_ple_gate": b_ple_gate[l].reshape(1, D),
        "w_ple_proj": w_ple_proj[l].astype(BF16),
        "g_final": g_final.reshape(1, D),
    }
    xs_list = (x_prompt, x_sample)
    ps_list = (p_prompt[l], p_sample[l])

    counts = jnp.zeros((1, LANES), F32)
    mixed = []
    route_ts = []
    for x in xs_list:
        h1, u2, route, route_t, counts = _mixer(x, wts, counts)
        mixed.append((h1, u2, route))
        route_ts.append(route_t)

    n_pairs = 2 * sum(x.shape[0] * x.shape[1] for x in xs_list)
    n_tiles = n_pairs // MOE_TILE + N_EXPERTS
    offsets, tile_expert, n_used, fill_starts = _sorted_layout(counts, n_tiles, MOE_TILE)
    recs = _position_records(route_ts, offsets)
    xs = _dispatch([u2 for _, u2, _ in mixed], recs, fill_starts, n_tiles * MOE_TILE,
                   DISPATCH_TILE, MOE_TILE)
    ys = _experts(xs, tile_expert, n_used, wts["w1"], wts["w3"], wts["w2"], MOE_TILE)

    outs = []
    for x, p, (h1, u2, route), rec in zip(xs_list, ps_list, mixed, recs):
        y = _ple(rec, ys, h1, route, p.reshape(h1.shape[0], -1),
                 wts["g_ple"], wts["w_ple_gate"], wts["b_ple_gate"], wts["w_ple_proj"], wts["g_final"],
                 PLE_TILE)
        outs.append(y.reshape(x.shape))
    return tuple(outs)
```

```python
import functools

import jax
import jax.numpy as jnp
import numpy as np
from jax import lax
from jax.experimental import pallas as pl
from jax.experimental.pallas import tpu as pltpu

HEAD_DIM = 128
A_HEADS = 8
A_WIDTH = A_HEADS * HEAD_DIM
B_HEADS = 8
B_KV_HEADS = 2
B_GROUP = B_HEADS // B_KV_HEADS
B_WIDTH = B_HEADS * HEAD_DIM
B_KV_WIDTH = B_KV_HEADS * HEAD_DIM
DILATED_CONFIGS = ((128, 1), (512, 4), (2048, 16))
B_RADIUS = 128
ROPE_THETA = 500000.0
ROPE_DIM = HEAD_DIM // 4
N_GROUPS = 4
EXPERTS_PER_GROUP = 8
N_EXPERTS = N_GROUPS * EXPERTS_PER_GROUP
RMS_EPS = 1e-6
NEG_INF = -1e30

LANES = 128
SUBLANES = 8
VMEM_LIMIT = 56 * 1024 * 1024

F32 = jnp.float32
BF16 = jnp.bfloat16


def _resident(shape):
    nd = len(shape)
    return pl.BlockSpec(shape, lambda *_: (0,) * nd, pipeline_mode=pl.Buffered(1))


def _params(sem):
    return pltpu.CompilerParams(dimension_semantics=sem, vmem_limit_bytes=VMEM_LIMIT)


def _rms(x, g):
    return x * lax.rsqrt(jnp.mean(x * x, axis=-1, keepdims=True) + RMS_EPS) * g


STRIDED_DILATIONS = tuple(d for _, d in DILATED_CONFIGS if d > 1)


def _qkv_kernel(x_ref, g_ref, w_ref, cos_ref, sa_ref, sb_ref, *rest):
    n_dil = len(STRIDED_DILATIONS)
    perm_refs = rest[:n_dil]
    qa_ref, ka_ref, va_ref, qb_ref, kb_ref, vb_ref = rest[n_dil:n_dil + 6]
    dil_refs = [rest[n_dil + 6 + i * n_dil:n_dil + 6 + (i + 1) * n_dil] for i in range(3)]
    tm = x_ref.shape[0]
    u = _rms(x_ref[...], g_ref[...]).astype(BF16)
    cos = cos_ref[...]
    sa = sa_ref[...]
    sb = sb_ref[...]
    half = ROPE_DIM // 2
    scale = HEAD_DIM ** -0.5

    def rope(t):
        return t * cos + pltpu.roll(t, half, 1) * sa + pltpu.roll(t, LANES - half, 1) * sb

    sections = (
        (qa_ref, 0, A_HEADS, True, scale, 0),
        (ka_ref, A_WIDTH, A_HEADS, True, None, 1),
        (va_ref, 2 * A_WIDTH, A_HEADS, False, None, 2),
        (qb_ref, 3 * A_WIDTH, B_HEADS, True, scale, None),
        (kb_ref, 3 * A_WIDTH + B_WIDTH, B_KV_HEADS, True, None, None),
        (vb_ref, 3 * A_WIDTH + B_WIDTH + B_KV_WIDTH, B_KV_HEADS, False, None, None),
    )
    for out_ref, c0, heads, rotary, sc, a_idx in sections:
        for h0 in range(0, heads, 4):
            nh = min(4, heads - h0)
            lo = c0 + h0 * HEAD_DIM
            acc = jnp.dot(u, w_ref[:, lo:lo + nh * HEAD_DIM], preferred_element_type=F32)
            for j in range(nh):
                t = acc[:, j * HEAD_DIM:(j + 1) * HEAD_DIM]
                if rotary:
                    t = rope(t)
                if sc is not None:
                    t = t * sc
                cols = slice((h0 + j) * HEAD_DIM, (h0 + j + 1) * HEAD_DIM)
                out_ref[:, cols] = t.astype(BF16)
        if a_idx is not None:
            for d, perm_ref, dref in zip(STRIDED_DILATIONS, perm_refs, dil_refs[a_idx]):
                n = tm // d
                permuted = jnp.dot(perm_ref[...], out_ref[...], preferred_element_type=F32).astype(BF16)
                for r in range(d):
                    dref[:, r * A_WIDTH:(r + 1) * A_WIDTH] = permuted[r * n:(r + 1) * n, :]


def _qkv(x2d, g, w_in, cos, sa, sb, seq, tm):
    T, D = x2d.shape
    n_pos = seq // tm
    row = lambda i: (i, 0)
    tab = pl.BlockSpec((tm, LANES), lambda i: (i % n_pos, 0))
    widths = (A_WIDTH, A_WIDTH, A_WIDTH, B_WIDTH, B_KV_WIDTH, B_KV_WIDTH)
    out_specs = [pl.BlockSpec((tm, w), row) for w in widths]
    out_shape = [jax.ShapeDtypeStruct((T, w), BF16) for w in widths]
    for _ in range(3):
        for d in STRIDED_DILATIONS:
            out_specs.append(pl.BlockSpec((tm // d, d * A_WIDTH), row))
            out_shape.append(jax.ShapeDtypeStruct((T // d, d * A_WIDTH), BF16))
    perms = [_residue_permutation(tm, d) for d in STRIDED_DILATIONS]
    return pl.pallas_call(
        _qkv_kernel,
        grid=(T // tm,),
        in_specs=[pl.BlockSpec((tm, D), row), _resident((1, D)), _resident(w_in.shape), tab, tab, tab,
                  *[_resident((tm, tm)) for _ in perms]],
        out_specs=out_specs,
        out_shape=out_shape,
        compiler_params=_params(("parallel",)),
        name="qkv",
    )(x2d, g, w_in, cos, sa, sb, *perms)


def _residue_permutation(n, d):
    dst = np.arange(n)
    src = (dst % (n // d)) * d + dst // (n // d)
    p = np.zeros((n, n), np.float32)
    p[dst, src] = 1.0
    return jnp.asarray(p, BF16)


ATTN_SUB = 128
ATTN_BLOCK = 1024


def _band_mask(first_q, sub, radius, length):
    shape = (sub, sub + 2 * radius)
    row = lax.broadcasted_iota(jnp.int32, shape, 0)
    col = lax.broadcasted_iota(jnp.int32, shape, 1)
    kpos = first_q - radius + col
    return (jnp.abs(col - radius - row) <= radius) & (kpos >= 0) & (kpos < length)


def _fill_window(ext_ref, prev_ref, cur_ref, next_ref, radius):
    n = cur_ref.shape[0]
    ext_ref[0:radius, :] = prev_ref[...]
    ext_ref[radius:radius + n, :] = cur_ref[...]
    ext_ref[radius + n:, :] = next_ref[...]


_NT = (((1,), (1,)), ((), ()))


def _attn_a_kernel(q_ref, kp_ref, kc_ref, kn_ref, vp_ref, vc_ref, vn_ref, o_ref, lse_ref, kx_ref, vx_ref,
                   *, length, lq, radius, n_res):
    sub = min(ATTN_SUB, lq)
    _fill_window(kx_ref, kp_ref, kc_ref, kn_ref, radius)
    _fill_window(vx_ref, vp_ref, vc_ref, vn_ref, radius)
    lane = lax.broadcasted_iota(jnp.int32, (sub, LANES), 1)
    for j in range(lq // sub):
        valid = _band_mask(pl.program_id(2) * lq + j * sub, sub, radius, length)
        q_rows = slice(j * sub, (j + 1) * sub)
        k_rows = slice(j * sub, (j + 1) * sub + 2 * radius)
        for res in range(n_res):
            cols = [slice(res * A_WIDTH + h * HEAD_DIM, res * A_WIDTH + (h + 1) * HEAD_DIM)
                    for h in range(A_HEADS)]
            q = jnp.stack([q_ref[q_rows, sl] for sl in cols], axis=0)
            k = jnp.stack([kx_ref[k_rows, sl] for sl in cols], axis=0)
            v = jnp.stack([vx_ref[k_rows, sl] for sl in cols], axis=0)
            s = jnp.einsum("hqd,hkd->hqk", q, k, preferred_element_type=F32)
            s = jnp.where(valid[None], s, NEG_INF)
            m = jnp.max(s, axis=-1, keepdims=True)
            p = jnp.exp(s - m)
            l = jnp.sum(p, axis=-1, keepdims=True)
            o = jnp.einsum("hqk,hkd->hqd", p.astype(BF16), v, preferred_element_type=F32) / l
            lse = m + jnp.log(l)
            lse_all = jnp.zeros((sub, LANES), F32)
            for h, sl in enumerate(cols):
                o_ref[q_rows, sl] = o[h].astype(o_ref.dtype)
                lse_all = jnp.where(lane == h, lse[h], lse_all)
            lse_ref[q_rows, res * LANES:(res + 1) * LANES] = lse_all


def _attn_a(qa, ka, va, batch, seq, dil, radius):
    ls = seq // dil
    lq = min(ATTN_BLOCK, ls)
    n_res = min(dil, ATTN_BLOCK // lq)
    per = lq // radius
    n_halo = ls // radius
    view = lambda t: t.reshape(batch, ls, dil * A_WIDTH)
    w = n_res * A_WIDTH
    cur = pl.BlockSpec((None, lq, w), lambda b, r, i: (b, i, r))
    prv = pl.BlockSpec((None, radius, w), lambda b, r, i: (b, jnp.maximum(i * per - 1, 0), r))
    nxt = pl.BlockSpec((None, radius, w), lambda b, r, i: (b, jnp.minimum((i + 1) * per, n_halo - 1), r))
    o, lse = pl.pallas_call(
        functools.partial(_attn_a_kernel, length=ls, lq=lq, radius=radius, n_res=n_res),
        grid=(batch, dil // n_res, ls // lq),
        in_specs=[cur, prv, cur, nxt, prv, cur, nxt],
        out_specs=[cur, pl.BlockSpec((None, lq, n_res * LANES), lambda b, r, i: (b, i, r))],
        out_shape=[jax.ShapeDtypeStruct((batch, ls, dil * A_WIDTH), BF16),
                   jax.ShapeDtypeStruct((batch, ls, dil * LANES), F32)],
        scratch_shapes=[pltpu.VMEM((lq + 2 * radius, w), BF16), pltpu.VMEM((lq + 2 * radius, w), BF16)],
        compiler_params=_params(("parallel", "parallel", "parallel")),
        name=f"attn_a_d{dil}",
    )(view(qa), view(ka), view(ka), view(ka), view(va), view(va), view(va))
    return o.reshape(batch * ls, dil * A_WIDTH), lse.reshape(batch * ls, dil * LANES)


def _attn_b_kernel(sink_ref, q_ref, kp_ref, kc_ref, kn_ref, vp_ref, vc_ref, vn_ref, g_ref, o_ref,
                   kx_ref, vx_ref, *, length, lq, radius):
    sub = ATTN_SUB
    _fill_window(kx_ref, kp_ref, kc_ref, kn_ref, radius)
    _fill_window(vx_ref, vp_ref, vc_ref, vn_ref, radius)
    for j in range(lq // sub):
        valid = _band_mask(pl.program_id(1) * lq + j * sub, sub, radius, length)
        q_rows = slice(j * sub, (j + 1) * sub)
        k_rows = slice(j * sub, (j + 1) * sub + 2 * radius)
        outs = []
        for g in range(B_KV_HEADS):
            kv = slice(g * HEAD_DIM, (g + 1) * HEAD_DIM)
            heads = range(g * B_GROUP, (g + 1) * B_GROUP)
            q = jnp.concatenate([q_ref[q_rows, h * HEAD_DIM:(h + 1) * HEAD_DIM] for h in heads], axis=0)
            s = lax.dot_general(q, kx_ref[k_rows, kv], _NT, preferred_element_type=F32)
            s = jnp.where(valid[None], s.reshape(B_GROUP, sub, sub + 2 * radius), NEG_INF)
            head = lax.broadcasted_iota(jnp.int32, (B_GROUP, 1, 1), 0)
            sink = jnp.zeros((B_GROUP, 1, 1), F32)
            for n, h in enumerate(heads):
                sink = jnp.where(head == n, sink_ref[h], sink)
            m = jnp.maximum(jnp.max(s, axis=-1, keepdims=True), sink)
            p = jnp.exp(s - m)
            l = jnp.sum(p, axis=-1, keepdims=True) + jnp.exp(sink - m)
            o = jnp.dot(p.reshape(B_GROUP * sub, sub + 2 * radius).astype(BF16), vx_ref[k_rows, kv],
                        preferred_element_type=F32).reshape(B_GROUP, sub, HEAD_DIM) / l
            outs.extend(o[n] for n in range(B_GROUP))
        o_ref[q_rows, :] = _rms(jnp.concatenate(outs, axis=-1), g_ref[...]).astype(o_ref.dtype)


def _attn_b(qb, kb, vb, sink, g_out_b, batch, seq):
    radius = B_RADIUS
    lq = min(ATTN_BLOCK, seq)
    per = lq // radius
    n_halo = seq // radius
    q3 = qb.reshape(batch, seq, B_WIDTH)
    k3 = kb.reshape(batch, seq, B_KV_WIDTH)
    v3 = vb.reshape(batch, seq, B_KV_WIDTH)
    qspec = pl.BlockSpec((None, lq, B_WIDTH), lambda b, i: (b, i, 0))
    cur = pl.BlockSpec((None, lq, B_KV_WIDTH), lambda b, i: (b, i, 0))
    prv = pl.BlockSpec((None, radius, B_KV_WIDTH), lambda b, i: (b, jnp.maximum(i * per - 1, 0), 0))
    nxt = pl.BlockSpec((None, radius, B_KV_WIDTH),
                       lambda b, i: (b, jnp.minimum((i + 1) * per, n_halo - 1), 0))
    ext = pltpu.VMEM((lq + 2 * radius, B_KV_WIDTH), BF16)
    out = pl.pallas_call(
        functools.partial(_attn_b_kernel, length=seq, lq=lq, radius=radius),
        grid=(batch, seq // lq),
        in_specs=[pl.BlockSpec(memory_space=pltpu.SMEM), qspec, prv, cur, nxt, prv, cur, nxt,
                  _resident((1, B_WIDTH))],
        out_specs=qspec,
        out_shape=jax.ShapeDtypeStruct((batch, seq, B_WIDTH), BF16),
        scratch_shapes=[ext, ext],
        compiler_params=_params(("parallel", "parallel")),
        name="attn_b",
    )(sink, q3, k3, k3, k3, v3, v3, v3, g_out_b)
    return out.reshape(batch * seq, B_WIDTH)


R_E1, R_E2, R_W1, R_W2, R_RANK1, R_RANK2 = range(6)


def _oproj_kernel(o1_ref, o4_ref, o16_ref, l1_ref, l4_ref, l16_ref, mb_ref, x_ref, wo_ref,
                  ga_ref, gf_ref, wr_ref, br_ref, cnt_in_ref, h_ref, u_ref, route_ref, route_t_ref, cnt_ref,
                  o4n_ref, o16n_ref, l4n_ref, l16n_ref, *, sub):
    tm = x_ref.shape[0]

    @pl.when(pl.program_id(0) == 0)
    def _():
        cnt_ref[...] = cnt_in_ref[...]

    for d, o_src, o_dst, l_src, l_dst in ((STRIDED_DILATIONS[0], o4_ref, o4n_ref, l4_ref, l4n_ref),
                                          (STRIDED_DILATIONS[1], o16_ref, o16n_ref, l16_ref, l16n_ref)):
        for r in range(d):
            rows = pl.ds(r, tm // d, stride=d)
            for h in range(A_HEADS):
                lo = r * A_WIDTH + h * HEAD_DIM
                o_dst[h, rows, :] = o_src[:, lo:lo + HEAD_DIM].astype(F32)
            l_dst[rows, :] = l_src[:, r * LANES:(r + 1) * LANES]

    for c in range(tm // sub):
        _oproj_rows(slice(c * sub, (c + 1) * sub), o1_ref, o4n_ref, o16n_ref, l1_ref, l4n_ref, l16n_ref,
                    mb_ref, x_ref, wo_ref, ga_ref, gf_ref, wr_ref, br_ref, h_ref, u_ref, route_ref,
                    route_t_ref, cnt_ref)


def _oproj_rows(rs, o1_ref, o2_ref, o3_ref, l1_ref, l2_ref, l3_ref, mb_ref, x_ref, wo_ref,
                ga_ref, gf_ref, wr_ref, br_ref, h_ref, u_ref, route_ref, route_t_ref, cnt_ref):
    tm = rs.stop - rs.start
    l1, l2, l3 = l1_ref[rs, :], l2_ref[rs, :], l3_ref[rs, :]
    mx = jnp.maximum(jnp.maximum(l1, l2), l3)
    e1, e2, e3 = jnp.exp(l1 - mx), jnp.exp(l2 - mx), jnp.exp(l3 - mx)
    inv = 1.0 / (e1 + e2 + e3)
    w1, w2, w3 = e1 * inv, e2 * inv, e3 * inv
    parts = []
    for h in range(A_HEADS):
        sl = slice(h * HEAD_DIM, (h + 1) * HEAD_DIM)
        bc = lambda w: jnp.broadcast_to(w[:, h:h + 1], (tm, HEAD_DIM))
        parts.append(bc(w1) * o1_ref[rs, sl].astype(F32) + bc(w2) * o2_ref[h, rs, :]
                     + bc(w3) * o3_ref[h, rs, :])
    mixed_a = _rms(jnp.concatenate(parts, axis=-1), ga_ref[...]).astype(BF16)
    h = (x_ref[rs, :]
         + jnp.dot(mixed_a, wo_ref[:A_WIDTH, :], preferred_element_type=F32)
         + jnp.dot(mb_ref[rs, :], wo_ref[A_WIDTH:, :], preferred_element_type=F32))
    h_ref[rs, :] = h
    u = _rms(h, gf_ref[...])
    u_ref[rs, :] = u

    u_hi = u.astype(BF16)
    u_lo = (u - u_hi.astype(F32)).astype(BF16)
    z = (jnp.dot(u_hi, wr_ref[...], preferred_element_type=F32)
         + jnp.dot(u_lo, wr_ref[...], preferred_element_type=F32))
    logits = z[:, :LANES] + z[:, LANES:] + br_ref[...]

    lt = logits.T
    el = lt[:N_EXPERTS, :]
    gl = lt[N_EXPERTS:N_EXPERTS + SUBLANES, :]
    g_row = lax.broadcasted_iota(jnp.int32, gl.shape, 0)
    e_row = lax.broadcasted_iota(jnp.int32, el.shape, 0)
    big = jnp.int32(LANES)
    is_group = g_row < N_GROUPS
    gmax = jnp.max(jnp.where(is_group, gl, -jnp.inf), axis=0, keepdims=True)
    gidx = jnp.min(jnp.where(is_group & (gl == gmax), g_row, big), axis=0, keepdims=True)
    ggate = 1.0 / jnp.sum(jnp.where(is_group, jnp.exp(gl - gmax), 0.0), axis=0, keepdims=True)
    in_group = (e_row >= gidx * EXPERTS_PER_GROUP) & (e_row < (gidx + 1) * EXPERTS_PER_GROUP)
    v1 = jnp.max(jnp.where(in_group, el, -jnp.inf), axis=0, keepdims=True)
    i1 = jnp.min(jnp.where(in_group & (el == v1), e_row, big), axis=0, keepdims=True)
    rest = in_group & (e_row != i1)
    v2 = jnp.max(jnp.where(rest, el, -jnp.inf), axis=0, keepdims=True)
    i2 = jnp.min(jnp.where(rest & (el == v2), e_row, big), axis=0, keepdims=True)
    t = jnp.exp(v2 - v1)
    tw1 = ggate / (1.0 + t)
    tw2 = ggate * t / (1.0 + t)

    onehot = ((e_row == i1) | (e_row == i2)).astype(F32)
    c_i = lax.broadcasted_iota(jnp.int32, (tm, tm), 0)
    r_i = lax.broadcasted_iota(jnp.int32, (tm, tm), 1)
    before = (c_i < r_i).astype(BF16)
    diag = (lax.broadcasted_iota(jnp.int32, (N_EXPERTS, LANES), 0)
            == lax.broadcasted_iota(jnp.int32, (N_EXPERTS, LANES), 1))
    cnt_col = jnp.sum(jnp.where(diag, cnt_ref[...], 0.0), axis=1, keepdims=True)
    prefix = jnp.dot(onehot.astype(BF16), before, preferred_element_type=F32) + cnt_col
    rank1 = jnp.sum(jnp.where(e_row == i1, prefix, 0.0), axis=0, keepdims=True)
    rank2 = jnp.sum(jnp.where(e_row == i2, prefix, 0.0), axis=0, keepdims=True)
    added = jnp.sum(onehot, axis=1, keepdims=True)
    cnt_ref[...] += jnp.sum(jnp.where(diag, added, 0.0), axis=0, keepdims=True)

    rec_row = lax.broadcasted_iota(jnp.int32, (LANES, tm), 0)
    rec_t = jnp.zeros((LANES, tm), F32)
    for slot, val in ((R_E1, i1.astype(F32)), (R_E2, i2.astype(F32)), (R_W1, tw1), (R_W2, tw2),
                      (R_RANK1, rank1), (R_RANK2, rank2)):
        rec_t = jnp.where(rec_row == slot, val, rec_t)
    route_t_ref[:, rs] = rec_t[:SUBLANES, :]
    route_ref[rs, :] = rec_t.T


def _oproj(o_list, lse_list, mixed_b, x2d, w_o, g_out_a, g_ffn, w_route, b_route, cnt_in, tm, sub):
    T, D = x2d.shape
    row = lambda i: (i, 0)
    dils = (1,) + STRIDED_DILATIONS
    aspecs = [pl.BlockSpec((tm // d, d * A_WIDTH), row) for d in dils]
    lspecs = [pl.BlockSpec((tm // d, d * LANES), row) for d in dils]
    lspec = pl.BlockSpec((tm, LANES), row)
    dspec = pl.BlockSpec((tm, D), row)
    cspec = pl.BlockSpec((1, LANES), lambda i: (0, 0))
    return pl.pallas_call(
        functools.partial(_oproj_kernel, sub=sub),
        grid=(T // tm,),
        in_specs=[*aspecs, *lspecs, pl.BlockSpec((tm, B_WIDTH), row), dspec,
                  _resident(w_o.shape), _resident((1, A_WIDTH)), _resident((1, D)),
                  _resident(w_route.shape), _resident((1, LANES)), cspec],
        out_specs=[dspec, dspec, lspec, pl.BlockSpec((SUBLANES, tm), lambda i: (0, i)), cspec],
        out_shape=[jax.ShapeDtypeStruct((T, D), F32), jax.ShapeDtypeStruct((T, D), F32),
                   jax.ShapeDtypeStruct((T, LANES), F32), jax.ShapeDtypeStruct((SUBLANES, T), F32),
                   jax.ShapeDtypeStruct((1, LANES), F32)],
        scratch_shapes=[pltpu.VMEM((A_HEADS, tm, HEAD_DIM), F32), pltpu.VMEM((A_HEADS, tm, HEAD_DIM), F32),
                        pltpu.VMEM((tm, LANES), F32), pltpu.VMEM((tm, LANES), F32)],
        compiler_params=_params(("arbitrary",)),
        name="oproj_route",
    )(*o_list, *lse_list, mixed_b, x2d, w_o, g_out_a, g_ffn, w_route, b_route, cnt_in)


def _row_copy(src_ref, src_row, dst_ref, dst_row, sem):
    return pltpu.make_async_copy(src_ref.at[pl.ds(src_row, 1)], dst_ref.at[pl.ds(dst_row, 1)], sem)


ROW1, ROW2 = R_E1, R_E2


def _position_records(route_ts, offsets):
    rec = jnp.concatenate(route_ts, axis=1).astype(jnp.int32)
    expert = jnp.stack([rec[R_E1], rec[R_E2]])
    rank = jnp.stack([rec[R_RANK1], rec[R_RANK2]])
    seg = jnp.zeros_like(expert)
    for e in range(N_EXPERTS):
        seg = jnp.where(expert == e, offsets[e], seg)
    rows = jnp.concatenate([seg + rank, jnp.zeros((SUBLANES - 2, rec.shape[1]), jnp.int32)], axis=0)
    sizes = np.cumsum([r.shape[1] for r in route_ts])[:-1]
    return jnp.split(rows, sizes, axis=1)


def _dispatch_kernel(fill_ref, *rest, tm, tiles_per_src, fill_rows):
    n_src = len(tiles_per_src)
    rec_refs = rest[:n_src]
    u_refs = rest[n_src:2 * n_src]
    xs_ref, sem, zero_ref, fill_sem = rest[2 * n_src:]
    i = pl.program_id(0)

    @pl.when(i == 0)
    def _():
        zero_ref[...] = jnp.zeros_like(zero_ref)

        def tile_copy(k):
            start = pl.multiple_of(fill_ref[k], fill_rows)
            return pltpu.make_async_copy(zero_ref, xs_ref.at[pl.ds(start, fill_rows)], fill_sem)

        for k in range(fill_ref.shape[0]):
            @pl.when(fill_ref[k] >= 0)
            def _():
                tile_copy(k).start()

        for k in range(fill_ref.shape[0]):
            @pl.when(fill_ref[k] >= 0)
            def _():
                tile_copy(k).wait()

    def scatter(rec_ref, u_ref):
        def issue(j, carry):
            for row in (rec_ref[ROW1, j], rec_ref[ROW2, j]):
                _row_copy(u_ref, j, xs_ref, row, sem).start()
            return carry

        lax.fori_loop(0, tm, issue, 0, unroll=8)

        def drain(j, carry):
            _row_copy(u_ref, 0, xs_ref, 0, sem).wait()
            return carry

        lax.fori_loop(0, 2 * tm, drain, 0, unroll=8)

    first = 0
    for rec_ref, u_ref, n in zip(rec_refs, u_refs, tiles_per_src):
        @pl.when((i >= first) & (i < first + n))
        def _():
            scatter(rec_ref, u_ref)

        first += n


def _dispatch(us, recs, fill_starts, n_rows, tm, fill_rows):
    D = us[0].shape[1]
    tiles_per_src = tuple(u.shape[0] // tm for u in us)
    in_specs = [pl.BlockSpec(memory_space=pltpu.SMEM)]
    starts = [sum(tiles_per_src[:k]) for k in range(len(us))]
    for first, n in zip(starts, tiles_per_src):
        in_specs.append(pl.BlockSpec((SUBLANES, tm), lambda i, first=first, n=n: (0, jnp.clip(i - first, 0, n - 1)),
                                     memory_space=pltpu.SMEM))
    for first, n in zip(starts, tiles_per_src):
        in_specs.append(pl.BlockSpec((tm, D), lambda i, first=first, n=n: (jnp.clip(i - first, 0, n - 1), 0)))
    first = sum(tiles_per_src)
    return pl.pallas_call(
        functools.partial(_dispatch_kernel, tm=tm, tiles_per_src=tiles_per_src, fill_rows=fill_rows),
        grid=(first,),
        in_specs=in_specs,
        out_specs=pl.BlockSpec(memory_space=pl.ANY),
        out_shape=jax.ShapeDtypeStruct((n_rows, D), F32),
        scratch_shapes=[pltpu.SemaphoreType.DMA(()), pltpu.VMEM((fill_rows, D), F32),
                        pltpu.SemaphoreType.DMA(())],
        compiler_params=_params(("arbitrary",)),
        name="moe_dispatch",
    )(fill_starts, *recs, *us)


def _experts_kernel(te_ref, used_ref, x_ref, w1_ref, w3_ref, w2_ref, y_ref, w1b_ref, w3b_ref, w2b_ref):
    i = pl.program_id(0)
    cur = jnp.minimum(i, used_ref[0] - 1)
    new_expert = (i == 0) | (te_ref[cur] != te_ref[jnp.maximum(cur - 1, 0)])

    @pl.when(new_expert & (i < used_ref[0]))
    def _():
        w1b_ref[...] = w1_ref[...].astype(BF16)
        w3b_ref[...] = w3_ref[...].astype(BF16)
        w2b_ref[...] = w2_ref[...].astype(BF16)

    @pl.when(i < used_ref[0])
    def _():
        x = x_ref[...].astype(BF16)
        a = jnp.dot(x, w1b_ref[...], preferred_element_type=F32)
        b = jnp.dot(x, w3b_ref[...], preferred_element_type=F32)
        hid = (a * jax.nn.sigmoid(a)) * b
        y_ref[...] = jnp.dot(hid.astype(BF16), w2b_ref[...], preferred_element_type=F32)

    @pl.when(i >= used_ref[0])
    def _():
        y_ref[...] = jnp.zeros_like(y_ref)


def _experts(xs, tile_expert, n_used, w1, w3, w2, tm):
    P, D = xs.shape
    F = w1.shape[2]
    row = lambda i, te, used: (jnp.minimum(i, used[0] - 1), 0)
    wmap = lambda i, te, used: (te[jnp.minimum(i, used[0] - 1)], 0, 0)
    grid_spec = pltpu.PrefetchScalarGridSpec(
        num_scalar_prefetch=2,
        grid=(P // tm,),
        in_specs=[pl.BlockSpec((tm, D), row),
                  pl.BlockSpec((None, D, F), wmap),
                  pl.BlockSpec((None, D, F), wmap),
                  pl.BlockSpec((None, F, D), wmap)],
        out_specs=pl.BlockSpec((tm, D), lambda i, te, used: (i, 0)),
        scratch_shapes=[pltpu.VMEM((D, F), BF16), pltpu.VMEM((D, F), BF16), pltpu.VMEM((F, D), BF16)],
    )
    return pl.pallas_call(
        _experts_kernel,
        grid_spec=grid_spec,
        out_shape=jax.ShapeDtypeStruct((P, D), F32),
        compiler_params=_params(("arbitrary",)),
        name="moe_experts",
    )(tile_expert, n_used, xs, w1, w3, w2)


def _ple_kernel(rec_ref, rec_next_ref, ys_ref, h_ref, route_ref, p_ref, gp_ref, wg_ref, bg_ref,
                wp_ref, gf_ref, o_ref, ybuf_a, ybuf_b, sems, *, tm):
    i = pl.program_id(0)
    half = tm // 2
    bufs = (ybuf_a, ybuf_b)

    def row_copies(rec_ref_, s, j, base):
        return (_row_copy(ys_ref, rec_ref_[ROW1, j], bufs[s].at[0], j - base, sems.at[s]),
                _row_copy(ys_ref, rec_ref_[ROW2, j], bufs[s].at[1], j - base, sems.at[s]))

    def drain(s):
        def wait(j, carry):
            _row_copy(ys_ref, 0, bufs[s].at[0], 0, sems.at[s]).wait()
            return carry

        lax.fori_loop(0, 2 * half, wait, 0, unroll=8)

    n_chunks = 4
    width = h_ref.shape[1] // n_chunks

    def compute(s, rows, requests):
        route = route_ref[rows, :]
        h = h_ref[rows, :] + route[:, R_W1:R_W1 + 1] * bufs[s][0] + route[:, R_W2:R_W2 + 1] * bufs[s][1]
        u = _rms(h, gp_ref[...]).astype(BF16)
        p = p_ref[rows, :].astype(BF16)
        per = len(requests) // n_chunks
        parts = []
        for c in range(n_chunks):
            for request in requests[c * per:(c + 1) * per]:
                for copy in row_copies(*request):
                    copy.start()
            cols = slice(c * width, (c + 1) * width)
            gate = jax.nn.sigmoid(jnp.dot(u, wg_ref[:, cols], preferred_element_type=F32) + bg_ref[:, cols])
            proj = jnp.dot(p, wp_ref[:, cols], preferred_element_type=F32)
            parts.append(h[:, cols] + gate * proj)
        o_ref[rows, :] = _rms(jnp.concatenate(parts, axis=-1), gf_ref[...])

    @pl.when(i == 0)
    def _():
        def issue(j, carry):
            for copy in row_copies(rec_ref, 0, j, 0):
                copy.start()
            return carry

        lax.fori_loop(0, half, issue, 0, unroll=8)

    drain(0)
    compute(0, slice(0, half), [(rec_ref, 1, j, half) for j in range(half, tm)])
    drain(1)
    compute(1, slice(half, tm), [(rec_next_ref, 0, j, 0) for j in range(half)])

    @pl.when(i + 1 == pl.num_programs(0))
    def _():
        drain(0)


def _ple(rec, ys, h, route, p2d, g_ple, w_gate, b_gate, w_proj, g_final, tm):
    T, D = h.shape
    P = p2d.shape[1]
    row = lambda i: (i, 0)
    n = T // tm
    return pl.pallas_call(
        functools.partial(_ple_kernel, tm=tm),
        grid=(n,),
        in_specs=[pl.BlockSpec((SUBLANES, tm), lambda i: (0, i), memory_space=pltpu.SMEM),
                  pl.BlockSpec((SUBLANES, tm), lambda i: (0, jnp.minimum(i + 1, n - 1)),
                               memory_space=pltpu.SMEM),
                  pl.BlockSpec(memory_space=pl.ANY),
                  pl.BlockSpec((tm, D), row), pl.BlockSpec((tm, LANES), row), pl.BlockSpec((tm, P), row),
                  _resident((1, D)), _resident(w_gate.shape), _resident((1, D)), _resident(w_proj.shape),
                  _resident((1, D))],
        out_specs=pl.BlockSpec((tm, D), row),
        out_shape=jax.ShapeDtypeStruct((T, D), F32),
        scratch_shapes=[pltpu.VMEM((2, tm // 2, D), F32), pltpu.VMEM((2, tm // 2, D), F32),
                        pltpu.SemaphoreType.DMA((2,))],
        compiler_params=_params(("arbitrary",)),
        name="ple_final",
    )(rec, rec, ys, h, route, p2d, g_ple, w_gate, b_gate, w_proj, g_final)


def _rope_tables(seq):
    half = ROPE_DIM // 2
    inv = ROPE_THETA ** (-(jnp.arange(half, dtype=F32) * 2.0) / ROPE_DIM)
    ang = jnp.arange(seq, dtype=F32)[:, None] * inv[None, :]
    cos, sin = jnp.cos(ang), jnp.sin(ang)
    zeros = jnp.zeros((seq, LANES - ROPE_DIM), F32)
    z16 = jnp.zeros((seq, half), F32)
    cos_t = jnp.concatenate([cos, cos, jnp.ones((seq, LANES - ROPE_DIM), F32)], axis=-1)
    sa_t = jnp.concatenate([z16, sin, zeros], axis=-1)
    sb_t = jnp.concatenate([-sin, z16, zeros], axis=-1)
    return cos_t, sa_t, sb_t


MOE_TILE = 512
OPROJ_TILE = 512
PLE_TILE = 512
DISPATCH_TILE = 1024


def _mixer(x, wts, cnt_in):
    batch, seq, D = x.shape
    T = batch * seq
    x2d = x.reshape(T, D)
    cos, sa, sb = _rope_tables(seq)
    qa, ka, va, qb, kb, vb, *strided = _qkv(x2d, wts["g_attn"], wts["w_in"], cos, sa, sb, seq, tm=256)
    n_dil = len(STRIDED_DILATIONS)
    qkv_by_dil = {1: (qa, ka, va)}
    for n, d in enumerate(STRIDED_DILATIONS):
        qkv_by_dil[d] = tuple(strided[i * n_dil + n] for i in range(3))
    o_list, lse_list = [], []
    for window, dil in DILATED_CONFIGS:
        o, lse = _attn_a(*qkv_by_dil[dil], batch, seq, dil, window // (2 * dil))
        o_list.append(o)
        lse_list.append(lse)
    mixed_b = _attn_b(qb, kb, vb, wts["sink"], wts["g_out_b"], batch, seq)
    return _oproj(o_list, lse_list, mixed_b, x2d, wts["w_o"], wts["g_out_a"], wts["g_ffn"],
                  wts["w_route"], wts["b_route"], cnt_in, tm=OPROJ_TILE, sub=OPROJ_TILE // 2)


def _sorted_layout(counts, n_tiles, tm):
    cnt = counts[0, :N_EXPERTS].astype(jnp.int32)
    tiles = (cnt + tm - 1) // tm
    ends = jnp.cumsum(tiles)
    offsets = (ends - tiles) * tm
    tile_expert = jnp.sum(jnp.arange(n_tiles, dtype=jnp.int32)[:, None] >= ends[None, :], axis=1)
    tile_expert = jnp.minimum(tile_expert, N_EXPERTS - 1).astype(jnp.int32)
    n_used = ends[-1]
    last_tile_start = jnp.where(tiles > 0, (ends - 1) * tm, -1)
    tail = n_used + jnp.arange(N_EXPERTS, dtype=jnp.int32)
    tail_start = jnp.where(tail < n_tiles, tail * tm, -1)
    fill_starts = jnp.concatenate([last_tile_start, tail_start]).astype(jnp.int32)
    return offsets, tile_expert, n_used.reshape(1).astype(jnp.int32), fill_starts


def kernel(x_prompt, x_sample, p_prompt, p_sample, g_attn, w_in, g_out_a, g_out_b, sink, w_o, g_ffn, w_group, b_group, w_router, b_router, w1, w3, w2, g_ple, w_ple_gate, b_ple_gate, w_ple_proj, g_final):
    depth = w_in.shape[0]
    assert depth == 1
    l = 0
    D = w_in.shape[1]
    pad = LANES - N_EXPERTS - N_GROUPS
    w_route = jnp.pad(jnp.concatenate([w_router[l], w_group[l]], axis=-1), ((0, 0), (0, pad)))
    w_route_hi = w_route.astype(BF16)
    w_route_lo = (w_route - w_route_hi.astype(F32)).astype(BF16)
    wts = {
        "g_attn": g_attn[l].reshape(1, D),
        "w_in": w_in[l].astype(BF16),
        "g_out_a": g_out_a[l].reshape(1, A_WIDTH),
        "g_out_b": g_out_b[l].reshape(1, B_WIDTH),
        "sink": sink[l],
        "w_o": w_o[l].astype(BF16),
        "g_ffn": g_ffn[l].reshape(1, D),
        "w_route": jnp.concatenate([w_route_hi, w_route_lo], axis=-1),
        "b_route": jnp.pad(jnp.concatenate([b_router[l], b_group[l]]), (0, pad)).reshape(1, LANES),
        "w1": w1[l],
        "w3": w3[l],
        "w2": w2[l],
        "g_ple": g_ple[l].reshape(1, D),
        "w_ple_gate": w_ple_gate[l].astype(BF16),
        "b_ple_gate": b_ple_gate[l].reshape(1, D),
        "w_ple_proj": w_ple_proj[l].astype(BF16),
        "g_final": g_final.reshape(1, D),
    }
    xs_list = (x_prompt, x_sample)
    ps_list = (p_prompt[l], p_sample[l])

    counts = jnp.zeros((1, LANES), F32)
    mixed = []
    route_ts = []
    for x in xs_list:
        h1, u2, route, route_t, counts = _mixer(x, wts, counts)
        mixed.append((h1, u2, route))
        route_ts.append(route_t)

    n_pairs = 2 * sum(x.shape[0] * x.shape[1] for x in xs_list)
    n_tiles = n_pairs // MOE_TILE + N_EXPERTS
    offsets, tile_expert, n_used, fill_starts = _sorted_layout(counts, n_tiles, MOE_TILE)
    recs = _position_records(route_ts, offsets)
    xs = _dispatch([u2 for _, u2, _ in mixed], recs, fill_starts, n_tiles * MOE_TILE,
                   DISPATCH_TILE, MOE_TILE)
    ys = _experts(xs, tile_expert, n_used, wts["w1"], wts["w3"], wts["w2"], MOE_TILE)

    outs = []
    for x, p, (h1, u2, route), rec in zip(xs_list, ps_list, mixed, recs):
        y = _ple(rec, ys, h1, route, p.reshape(h1.shape[0], -1),
                 wts["g_ple"], wts["w_ple_gate"], wts["b_ple_gate"], wts["w_ple_proj"], wts["g_final"],
                 PLE_TILE)
        outs.append(y.reshape(x.shape))
    return tuple(outs)
```
